```python
import math
import jax, jax.numpy as jnp
from jax import lax
import numpy as np

D_MODEL = 2048
BATCH = 4
SEQ = 2048
DEPTH = 2
DEC_BATCH = 128
DEC_SEQ = 8
PAST_LEN = 16384
PAGE_SIZE = 128

WC = D_MODEL // 2
CONV_W = 3
WS = D_MODEL // 2
SSM_H = 16
SSM_G = WS // SSM_H
SSM_P = 64
DT_MIN = 1e-3
DT_MAX = 1e-1
N_PROJ = 3 * WC + WS + 2 * D_MODEL
N_GROUPS = 4
EXPERTS_PER_GROUP = 8
N_EXPERTS = N_GROUPS * EXPERTS_PER_GROUP
TOP_K = 2
D_EXPERT = D_MODEL // 4
EPS = 1e-6

kernel_name = "hybrid_conv_s5_hmoe_decode_step"


def rmsnorm(x, g):
    xf = x.astype(jnp.float32)
    out = xf * lax.rsqrt(jnp.mean(xf * xf, axis=-1, keepdims=True) + EPS) * g.astype(jnp.float32)
    return out.astype(x.dtype)


def ssm_discretize(lam_re, lam_im, log_step, b_re, b_im):
    lr = lam_re.astype(jnp.float32)
    li = lam_im.astype(jnp.float32)
    dt = jnp.exp(log_step.astype(jnp.float32))[:, None]
    mag = jnp.exp(lr * dt)
    lb_re = mag * jnp.cos(li * dt)
    lb_im = mag * jnp.sin(li * dt)
    nr, ni = lb_re - 1.0, lb_im
    den = lr * lr + li * li
    f_re = (nr * lr + ni * li) / den
    f_im = (ni * lr - nr * li) / den
    br = b_re.astype(jnp.float32)
    bi = b_im.astype(jnp.float32)
    bb_re = f_re[..., None] * br - f_im[..., None] * bi
    bb_im = f_re[..., None] * bi + f_im[..., None] * br
    return lb_re, lb_im, bb_re, bb_im


def _complex_scan_op(e1, e2):
    a1r, a1i, b1r, b1i = e1
    a2r, a2i, b2r, b2i = e2
    ar = a1r * a2r - a1i * a2i
    ai = a1r * a2i + a1i * a2r
    br = a2r * b1r - a2i * b1i + b2r
    bi = a2r * b1i + a2i * b1r + b2i
    return (ar, ai, br, bi)


def ssm_apply(u, h0_re, h0_im, lb_re, lb_im, bb_re, bb_im, c_re, c_im, d):
    n, l, _ = u.shape
    ug = u.reshape(n, l, SSM_G, SSM_H)
    bu_re = jnp.einsum('nlgh,gph->nlgp', ug, bb_re)
    bu_im = jnp.einsum('nlgh,gph->nlgp', ug, bb_im)
    a_re = jnp.broadcast_to(lb_re, bu_re.shape)
    a_im = jnp.broadcast_to(lb_im, bu_im.shape)
    A_re, A_im, S_re, S_im = lax.associative_scan(_complex_scan_op, (a_re, a_im, bu_re, bu_im), axis=1)
    h0r = h0_re.astype(jnp.float32)[:, None]
    h0i = h0_im.astype(jnp.float32)[:, None]
    x_re = A_re * h0r - A_im * h0i + S_re
    x_im = A_re * h0i + A_im * h0r + S_im
    y = (jnp.einsum('nlgp,ghp->nlgh', x_re, c_re.astype(jnp.float32))
         - jnp.einsum('nlgp,ghp->nlgh', x_im, c_im.astype(jnp.float32)))
    y = y.reshape(n, l, WS) + d.astype(jnp.float32) * u
    return y, x_re[:, -1], x_im[:, -1]


def hier_moe(xn, w_rg, b_rg, w_re, b_re, w_gate, w_up, w_down):
    n, l, dm = xn.shape
    t = xn.reshape(n * l, dm)
    lg = (t @ w_rg).astype(jnp.float32) + b_rg.astype(jnp.float32)
    pg = jax.nn.softmax(lg, axis=-1)
    gp, gi = lax.top_k(pg, 1)
    le = ((t @ w_re).astype(jnp.float32) + b_re.astype(jnp.float32)).reshape(-1, N_GROUPS, EXPERTS_PER_GROUP)
    le_sel = jnp.einsum('tge,tg->te', le, jax.nn.one_hot(gi[:, 0], N_GROUPS, dtype=jnp.float32))
    tv, ti = lax.top_k(le_sel, TOP_K)
    w = jax.nn.softmax(tv, axis=-1) * gp
    eid = gi * EXPERTS_PER_GROUP + ti
    comb = jnp.sum(jax.nn.one_hot(eid, N_EXPERTS, dtype=jnp.float32) * w[..., None], axis=1)
    hg = jnp.einsum('td,edf->tef', t, w_gate)
    hu = jnp.einsum('td,edf->tef', t, w_up)
    h = jax.nn.silu(hg) * hu * comb[:, :, None].astype(t.dtype)
    out = jnp.einsum('tef,efd->td', h, w_down)
    return out.reshape(n, l, dm).astype(xn.dtype)


def trunk_layer(x, conv_buf, h_re, h_im, norm_mix, w_in, b_gate, conv_w, w_out_conv,
                disc, c_re, c_im, d, w_glu, w_o, norm_ffn, w_rg, b_rg, w_re, b_re,
                w_gate, w_up, w_down):
    l = x.shape[1]
    xn = rmsnorm(x, norm_mix)
    proj = xn @ w_in
    b = proj[..., :WC]
    c = proj[..., WC:2 * WC]
    hc = proj[..., 2 * WC:3 * WC]
    u = proj[..., 3 * WC:3 * WC + WS]
    gate = proj[..., 3 * WC + WS:]
    v = c * hc
    vv = jnp.concatenate([conv_buf.astype(v.dtype), v], axis=1)
    conv = conv_w[0] * vv[:, :l] + conv_w[1] * vv[:, 1:l + 1] + conv_w[2] * vv[:, 2:l + 2]
    ya = (b * conv) @ w_out_conv
    lb_re, lb_im, bb_re, bb_im = disc
    s, hr_new, hi_new = ssm_apply(u.astype(jnp.float32), h_re, h_im, lb_re, lb_im, bb_re, bb_im, c_re, c_im, d)
    z = jax.nn.gelu(s).astype(x.dtype)
    glu = z @ w_glu
    yb = glu[..., :D_MODEL] * jax.nn.sigmoid(glu[..., D_MODEL:])
    g = jax.nn.sigmoid(gate + b_gate)
    merged = g[..., :D_MODEL] * ya + g[..., D_MODEL:] * yb
    x = x + (merged @ w_o).astype(x.dtype)
    x = x + hier_moe(rmsnorm(x, norm_ffn), w_rg, b_rg, w_re, b_re, w_gate, w_up, w_down)
    return x, vv[:, -(CONV_W - 1):], hr_new, hi_new


def setup_inputs(seed: int = 0) -> dict:
    key = jax.random.key(seed)
    ks = jax.random.split(key, 32)
    f32 = jnp.float32
    nrm = lambda k, s, sc: jax.random.normal(k, s, f32) * sc
    lam_im = (math.pi * jnp.arange(SSM_P, dtype=f32))[None, None, :] + nrm(ks[8], (DEPTH, SSM_G, SSM_P), 0.01)
    return {
        "x_prompt": nrm(ks[0], (BATCH, SEQ, D_MODEL), 1.0),
        "x_sample": nrm(ks[1], (DEC_BATCH, DEC_SEQ, D_MODEL), 1.0),
        "state_conv": nrm(ks[2], (DEPTH, DEC_BATCH, CONV_W - 1, WC), 1.0),
        "state_ssm_re": nrm(ks[3], (DEPTH, DEC_BATCH, SSM_G, SSM_P), 0.3),
        "state_ssm_im": nrm(ks[4], (DEPTH, DEC_BATCH, SSM_G, SSM_P), 0.3),
        "norm_mix": 1.0 + nrm(ks[5], (DEPTH, D_MODEL), 0.02),
        "w_in": nrm(ks[6], (DEPTH, D_MODEL, N_PROJ), D_MODEL ** -0.5),
        "b_gate": nrm(ks[7], (DEPTH, 2 * D_MODEL), 0.02),
        "conv_w": nrm(ks[9], (DEPTH, CONV_W, WC), CONV_W ** -0.5),
        "w_out_conv": nrm(ks[10], (DEPTH, WC, D_MODEL), WC ** -0.5),
        "ssm_lambda_re": -0.5 + nrm(ks[11], (DEPTH, SSM_G, SSM_P), 0.01),
        "ssm_lambda_im": lam_im,
        "ssm_log_step": jax.random.uniform(ks[12], (DEPTH, SSM_G), f32, math.log(DT_MIN), math.log(DT_MAX)),
        "ssm_b_re": nrm(ks[13], (DEPTH, SSM_G, SSM_P, SSM_H), (2 * SSM_H) ** -0.5),
        "ssm_b_im": nrm(ks[14], (DEPTH, SSM_G, SSM_P, SSM_H), (2 * SSM_H) ** -0.5),
        "ssm_c_re": nrm(ks[15], (DEPTH, SSM_G, SSM_H, SSM_P), (2 * SSM_P) ** -0.5),
        "ssm_c_im": nrm(ks[16], (DEPTH, SSM_G, SSM_H, SSM_P), (2 * SSM_P) ** -0.5),
        "ssm_d": nrm(ks[17], (DEPTH, WS), 1.0),
        "w_glu": nrm(ks[18], (DEPTH, WS, 2 * D_MODEL), WS ** -0.5),
        "w_o": nrm(ks[19], (DEPTH, D_MODEL, D_MODEL), D_MODEL ** -0.5),
        "norm_ffn": 1.0 + nrm(ks[20], (DEPTH, D_MODEL), 0.02),
        "w_router_group": nrm(ks[21], (DEPTH, D_MODEL, N_GROUPS), D_MODEL ** -0.5),
        "b_router_group": nrm(ks[22], (DEPTH, N_GROUPS), 0.01),
        "w_router_expert": nrm(ks[23], (DEPTH, D_MODEL, N_EXPERTS), D_MODEL ** -0.5),
        "b_router_expert": nrm(ks[24], (DEPTH, N_EXPERTS), 0.01),
        "w_exp_gate": nrm(ks[25], (DEPTH, N_EXPERTS, D_MODEL, D_EXPERT), D_MODEL ** -0.5),
        "w_exp_up": nrm(ks[26], (DEPTH, N_EXPERTS, D_MODEL, D_EXPERT), D_MODEL ** -0.5),
        "w_exp_down": nrm(ks[27], (DEPTH, N_EXPERTS, D_EXPERT, D_MODEL), D_EXPERT ** -0.5),
        "norm_final": 1.0 + nrm(ks[28], (D_MODEL,), 0.02),
    }


def reference(x_prompt, x_sample, state_conv, state_ssm_re, state_ssm_im,
              norm_mix, w_in, b_gate, conv_w, w_out_conv,
              ssm_lambda_re, ssm_lambda_im, ssm_log_step, ssm_b_re, ssm_b_im,
              ssm_c_re, ssm_c_im, ssm_d, w_glu, w_o, norm_ffn,
              w_router_group, b_router_group, w_router_expert, b_router_expert,
              w_exp_gate, w_exp_up, w_exp_down, norm_final):
    nb = x_prompt.shape[0]
    xp = x_prompt
    xs = x_sample
    conv_p, re_p, im_p = [], [], []
    conv_s, re_s, im_s = [], [], []
    zero_conv = jnp.zeros((nb, CONV_W - 1, WC), x_prompt.dtype)
    zero_h = jnp.zeros((nb, SSM_G, SSM_P), jnp.float32)
    for i in range(DEPTH):
        disc = ssm_discretize(ssm_lambda_re[i], ssm_lambda_im[i], ssm_log_step[i], ssm_b_re[i], ssm_b_im[i])
        shared = (norm_mix[i], w_in[i], b_gate[i], conv_w[i], w_out_conv[i], disc,
                  ssm_c_re[i], ssm_c_im[i], ssm_d[i], w_glu[i], w_o[i], norm_ffn[i],
                  w_router_group[i], b_router_group[i], w_router_expert[i], b_router_expert[i],
                  w_exp_gate[i], w_exp_up[i], w_exp_down[i])
        xp, cb, hr, hi = trunk_layer(xp, zero_conv, zero_h, zero_h, *shared)
        conv_p.append(cb); re_p.append(hr); im_p.append(hi)
        xs, cb, hr, hi = trunk_layer(xs, state_conv[i], state_ssm_re[i], state_ssm_im[i], *shared)
        conv_s.append(cb); re_s.append(hr); im_s.append(hi)
    y_prompt = rmsnorm(xp, norm_final)
    y_sample = rmsnorm(xs, norm_final)
    return (y_prompt, y_sample,
            jnp.stack(conv_p), jnp.stack(re_p), jnp.stack(im_p),
            jnp.stack(conv_s), jnp.stack(re_s), jnp.stack(im_s))
```

```python
import functools

import jax
import jax.numpy as jnp
from jax import lax
from jax.experimental import pallas as pl
from jax.experimental.pallas import tpu as pltpu

F32 = jnp.float32
BF16 = jnp.bfloat16
EPS = 1e-6

VMEM_LIMIT_BYTES = 56 * 1024 * 1024
SUBLANES = 8
LANES = 128

D_MODEL = 2048
WC = 1024
WS = 1024
SSM_H = 16
SSM_G = 64
SSM_P = 64
N_PROJ = 3 * WC + WS + 2 * D_MODEL
N_GROUPS = 4
EXPERTS_PER_GROUP = 8
N_EXPERTS = N_GROUPS * EXPERTS_PER_GROUP
TOP_K = 2
D_EXPERT = 512

GROUPS_PER_SLICE = LANES // SSM_H
N_SLICES = SSM_G // GROUPS_PER_SLICE
SLICE_STATES = GROUPS_PER_SLICE * SSM_P
SLICE_COLS = 2 * SLICE_STATES

TM_PROJ = 1024
TN_PROJ = 512
TM_CONV = 512
TN_CONV = 512
TM_SSM = 1024
TM_MERGE = 1024
TJ_MERGE = 512
TM_ROUTE = 512
TM_DISPATCH = 512
TE = 256
TM_COMBINE = 256


def _cparams(*sem):
    return pltpu.CompilerParams(dimension_semantics=sem, vmem_limit_bytes=VMEM_LIMIT_BYTES)


def _rmsnorm(x, g):
    ms = jnp.mean(x * x, axis=-1, keepdims=True)
    return x * lax.rsqrt(ms + EPS) * g


def _in_proj_kernel(x_ref, g_ref, w_ref, p_ref, u_ref, xn_ref, *, u_first, u_last):
    j = pl.program_id(1)

    @pl.when(j == 0)
    def _():
        xn_ref[...] = _rmsnorm(x_ref[...], g_ref[...]).astype(BF16)

    acc = jnp.dot(xn_ref[...], w_ref[...].astype(BF16), preferred_element_type=F32)
    p_ref[...] = acc

    @pl.when((j >= u_first) & (j <= u_last))
    def _():
        u_ref[...] = acc


def _in_proj(x, norm_g, w_in, layer, u_shape, u_index, name):
    n_row_blks = x.shape[0] // TM_PROJ
    u_first = (3 * WC) // TN_PROJ
    u_last = (3 * WC + WS) // TN_PROJ - 1
    kern = functools.partial(_in_proj_kernel, u_first=u_first, u_last=u_last)
    return pl.pallas_call(
        kern,
        grid=(n_row_blks, N_PROJ // TN_PROJ),
        in_specs=[
            pl.BlockSpec((TM_PROJ, D_MODEL), lambda i, j: (i, 0)),
            pl.BlockSpec((None, 1, D_MODEL), lambda i, j: (layer, 0, 0)),
            pl.BlockSpec((None, D_MODEL, TN_PROJ), lambda i, j: (layer, 0, j)),
        ],
        out_specs=[
            pl.BlockSpec((TM_PROJ, TN_PROJ), lambda i, j: (i, j)),
            pl.BlockSpec((TM_PROJ, TN_PROJ), functools.partial(u_index, u_first=u_first)),
        ],
        out_shape=[
            jax.ShapeDtypeStruct((n_row_blks * TM_PROJ, N_PROJ), F32),
            jax.ShapeDtypeStruct(u_shape, F32),
        ],
        scratch_shapes=[pltpu.VMEM((TM_PROJ, D_MODEL), BF16)],
        compiler_params=_cparams("arbitrary", "arbitrary"),
        name=name,
    )(x, norm_g, w_in)


def _conv_kernel(b_ref, c_ref, h_ref, init_ref, cw_ref, w_ref, ya_ref, st_ref, bc_ref, carry_ref,
                 *, shift, tiles_per_seq):
    i = pl.program_id(0)
    j = pl.program_id(1)
    tm = b_ref.shape[0]
    nc = carry_ref.shape[0]

    @pl.when(j == 0)
    def _():
        @pl.when(i % tiles_per_seq == 0)
        def _():
            carry_ref[...] = init_ref[...]

        v = c_ref[...] * h_ref[...]
        ext = jnp.concatenate([carry_ref[...], v], axis=0)
        v1 = ext[nc - shift:nc - shift + tm]
        v2 = ext[nc - 2 * shift:nc - 2 * shift + tm]
        cw = cw_ref[...]
        conv = cw[0:1] * v2 + cw[1:2] * v1 + cw[2:3] * v
        bc_ref[...] = (b_ref[...] * conv).astype(BF16)
        carry_ref[...] = v[tm - nc:]
        st_ref[...] = v[tm - nc:]

    ya_ref[...] = jnp.dot(bc_ref[...], w_ref[...].astype(BF16), preferred_element_type=F32)


def _conv_branch(proj, init, conv_w, w_out_conv, layer, tm, shift, tiles_per_seq, name):
    n_rows = proj.shape[0]
    nc = init.shape[1]
    n_seq = init.shape[0]
    kern = functools.partial(_conv_kernel, shift=shift, tiles_per_seq=tiles_per_seq)
    return pl.pallas_call(
        kern,
        grid=(n_rows // tm, D_MODEL // TN_CONV),
        in_specs=[
            pl.BlockSpec((tm, WC), lambda i, j: (i, 0)),
            pl.BlockSpec((tm, WC), lambda i, j: (i, 1)),
            pl.BlockSpec((tm, WC), lambda i, j: (i, 2)),
            pl.BlockSpec((None, nc, WC), lambda i, j: (i // tiles_per_seq, 0, 0)),
            pl.BlockSpec((None, 3, WC), lambda i, j: (layer, 0, 0)),
            pl.BlockSpec((None, WC, TN_CONV), lambda i, j: (layer, 0, j)),
        ],
        out_specs=[
            pl.BlockSpec((tm, TN_CONV), lambda i, j: (i, j)),
            pl.BlockSpec((None, nc, WC), lambda i, j: (i // tiles_per_seq, 0, 0)),
        ],
        out_shape=[
            jax.ShapeDtypeStruct((n_rows, D_MODEL), F32),
            jax.ShapeDtypeStruct((n_seq, nc, WC), F32),
        ],
        scratch_shapes=[pltpu.VMEM((tm, WC), BF16), pltpu.VMEM((nc, WC), F32)],
        compiler_params=_cparams("arbitrary", "arbitrary"),
        name=name,
    )(proj, proj, proj, init, conv_w, w_out_conv)


def _disc_kernel(lr_ref, li_ref, ls_ref, br_ref, bi_ref, lb_ref, a2_ref, bb_ref, ab_ref):
    lr = lr_ref[...]
    li = li_ref[...]
    dt = jnp.exp(ls_ref[...])
    mag = jnp.exp(lr * dt)
    lbr = mag * jnp.cos(li * dt)
    lbi = mag * jnp.sin(li * dt)
    nr, ni = lbr - 1.0, lbi
    den = lr * lr + li * li
    fr = (nr * lr + ni * li) / den
    fi = (ni * lr - nr * li) / den
    lb_ref[0] = lbr
    lb_ref[1] = lbi
    a2_ref[0] = lbr * lbr - lbi * lbi
    a2_ref[1] = 2.0 * lbr * lbi
    frt = jnp.concatenate([fr] * SSM_H, axis=0)
    fit = jnp.concatenate([fi] * SSM_H, axis=0)
    lbrt = jnp.concatenate([lbr] * SSM_H, axis=0)
    lbit = jnp.concatenate([lbi] * SSM_H, axis=0)
    br = br_ref[...]
    bi = bi_ref[...]
    bbr = frt * br - fit * bi
    bbi = frt * bi + fit * br
    bb_ref[0] = bbr
    bb_ref[1] = bbi
    ab_ref[0] = lbrt * bbr - lbit * bbi
    ab_ref[1] = lbrt * bbi + lbit * bbr


def _discretize(lam_re, lam_im, log_step, b_re, b_im):
    g, p = lam_re.shape
    hg = SSM_H * g
    brt = jnp.transpose(b_re, (2, 0, 1)).reshape(hg, p)
    bit = jnp.transpose(b_im, (2, 0, 1)).reshape(hg, p)
    return pl.pallas_call(
        _disc_kernel,
        out_shape=[
            jax.ShapeDtypeStruct((2, g, p), F32),
            jax.ShapeDtypeStruct((2, g, p), F32),
            jax.ShapeDtypeStruct((2, hg, p), F32),
            jax.ShapeDtypeStruct((2, hg, p), F32),
        ],
        name="s5_discretize",
    )(lam_re, lam_im, log_step.reshape(g, 1), brt, bit)


def _block_diag_in(m):
    m = m.reshape(SSM_H, N_SLICES, GROUPS_PER_SLICE, SSM_P)
    eye = jnp.eye(GROUPS_PER_SLICE, dtype=m.dtype)
    out = jnp.einsum('hkgp,gq->kghqp', m, eye)
    return out.reshape(N_SLICES, LANES, SLICE_STATES)


def _block_diag_out(c):
    c = c.reshape(N_SLICES, GROUPS_PER_SLICE, SSM_H, SSM_P)
    eye = jnp.eye(GROUPS_PER_SLICE, dtype=c.dtype)
    out = jnp.einsum('kghp,gq->kgpqh', c, eye)
    return out.reshape(N_SLICES, SLICE_STATES, LANES)


def _slice_rows(a):
    a = a.reshape(2, N_SLICES, SLICE_STATES)
    a = jnp.transpose(a, (1, 0, 2)).reshape(N_SLICES, 1, SLICE_COLS)
    return jnp.broadcast_to(a, (N_SLICES, SUBLANES, SLICE_COLS))


def _gelu_out(x_state, u, c_ref, d_ref):
    y = jnp.dot(x_state.astype(BF16), c_ref[...], preferred_element_type=F32)
    s = y + d_ref[...] * u
    return jax.nn.gelu(s).astype(BF16)


def _ssm_prompt_kernel(u_ref, wb_ref, a_ref, c_ref, d_ref, z_ref, hs_ref, v_ref, st_ref, up_ref, *, nb):
    tb = pl.program_id(1)
    tm = u_ref.shape[0]
    half = SLICE_STATES

    @pl.when(tb == 0)
    def _():
        st_ref[...] = jnp.zeros_like(st_ref)
        up_ref[...] = jnp.zeros_like(up_ref)

    u = u_ref[...]
    ext = jnp.concatenate([up_ref[...], u], axis=0)
    u_prev = ext[SUBLANES - nb:SUBLANES - nb + tm]
    lhs = jnp.concatenate([u, u_prev], axis=1).astype(BF16)
    v_ref[...] = jnp.dot(lhs, wb_ref[...], preferred_element_type=F32)
    up_ref[...] = u[tm - SUBLANES:]

    a_re = a_ref[:, :half]
    a_im = a_ref[:, half:]

    def body(s, carry):
        sre, sim = carry
        r0 = pl.multiple_of(s * SUBLANES, SUBLANES)
        vre = v_ref[pl.ds(r0, SUBLANES), :half]
        vim = v_ref[pl.ds(r0, SUBLANES), half:]
        nre = a_re * sre - a_im * sim + vre
        nim = a_re * sim + a_im * sre + vim
        v_ref[pl.ds(r0, SUBLANES), :half] = nre
        v_ref[pl.ds(r0, SUBLANES), half:] = nim
        return nre, nim

    sre, sim = lax.fori_loop(0, tm // SUBLANES, body, (st_ref[:, :half], st_ref[:, half:]))
    st_ref[:, :half] = sre
    st_ref[:, half:] = sim
    hs_ref[:, :half] = sre
    hs_ref[:, half:] = sim
    z_ref[...] = _gelu_out(v_ref[...], u, c_ref, d_ref)


def _ssm_prompt(u_tm, wb, a2, cmat, dvec, nb):
    n_rows = u_tm.shape[0]
    kern = functools.partial(_ssm_prompt_kernel, nb=nb)
    return pl.pallas_call(
        kern,
        grid=(N_SLICES, n_rows // TM_SSM),
        in_specs=[
            pl.BlockSpec((TM_SSM, LANES), lambda k, t: (t, k)),
            pl.BlockSpec((None, 2 * LANES, SLICE_COLS), lambda k, t: (k, 0, 0)),
            pl.BlockSpec((None, SUBLANES, SLICE_COLS), lambda k, t: (k, 0, 0)),
            pl.BlockSpec((None, SLICE_COLS, LANES), lambda k, t: (k, 0, 0)),
            pl.BlockSpec((None, 1, LANES), lambda k, t: (k, 0, 0)),
        ],
        out_specs=[
            pl.BlockSpec((TM_SSM, LANES), lambda k, t: (t, k)),
            pl.BlockSpec((None, SUBLANES, SLICE_COLS), lambda k, t: (k, 0, 0)),
        ],
        out_shape=[
            jax.ShapeDtypeStruct((n_rows, WS), BF16),
            jax.ShapeDtypeStruct((N_SLICES, SUBLANES, SLICE_COLS), F32),
        ],
        scratch_shapes=[
            pltpu.VMEM((TM_SSM, SLICE_COLS), F32),
            pltpu.VMEM((SUBLANES, SLICE_COLS), F32),
            pltpu.VMEM((SUBLANES, LANES), F32),
        ],
        compiler_params=_cparams("arbitrary", "arbitrary"),
        name="s5_prompt",
    )(u_tm, wb, a2, cmat, dvec)


def _ssm_sample_kernel(u_ref, h0_ref, wb_ref, a_ref, c_ref, d_ref, z_ref, hs_ref, v_ref, *, n_seq, n_steps):
    half = SLICE_STATES
    u = u_ref[...]
    v_ref[...] = jnp.dot(u.astype(BF16), wb_ref[...], preferred_element_type=F32)
    a_re = a_ref[:, :half]
    a_im = a_ref[:, half:]

    def body(q, _):
        n0 = pl.multiple_of(q * SUBLANES, SUBLANES)
        sre = h0_ref[pl.ds(n0, SUBLANES), :half]
        sim = h0_ref[pl.ds(n0, SUBLANES), half:]
        for t in range(n_steps):
            r0 = pl.multiple_of(t * n_seq + n0, SUBLANES)
            vre = v_ref[pl.ds(r0, SUBLANES), :half]
            vim = v_ref[pl.ds(r0, SUBLANES), half:]
            nre = a_re * sre - a_im * sim + vre
            nim = a_re * sim + a_im * sre + vim
            v_ref[pl.ds(r0, SUBLANES), :half] = nre
            v_ref[pl.ds(r0, SUBLANES), half:] = nim
            sre, sim = nre, nim
        hs_ref[pl.ds(n0, SUBLANES), :half] = sre
        hs_ref[pl.ds(n0, SUBLANES), half:] = sim
        return 0

    lax.fori_loop(0, n_seq // SUBLANES, body, 0)
    z_ref[...] = _gelu_out(v_ref[...], u, c_ref, d_ref)


def _ssm_sample(u_s, h0, wb1, a1, cmat, dvec, n_seq, n_steps):
    n_rows = u_s.shape[0]
    kern = functools.partial(_ssm_sample_kernel, n_seq=n_seq, n_steps=n_steps)
    return pl.pallas_call(
        kern,
        grid=(N_SLICES,),
        in_specs=[
            pl.BlockSpec((n_rows, LANES), lambda k: (0, k)),
            pl.BlockSpec((n_seq, SLICE_COLS), lambda k: (0, k)),
            pl.BlockSpec((None, LANES, SLICE_COLS), lambda k: (k, 0, 0)),
            pl.BlockSpec((None, SUBLANES, SLICE_COLS), lambda k: (k, 0, 0)),
            pl.BlockSpec((None, SLICE_COLS, LANES), lambda k: (k, 0, 0)),
            pl.BlockSpec((None, 1, LANES), lambda k: (k, 0, 0)),
        ],
        out_specs=[
            pl.BlockSpec((n_rows, LANES), lambda k: (0, k)),
            pl.BlockSpec((n_seq, SLICE_COLS), lambda k: (0, k)),
        ],
        out_shape=[
            jax.ShapeDtypeStruct((n_rows, WS), BF16),
            jax.ShapeDtypeStruct((n_seq, N_SLICES * SLICE_COLS), F32),
        ],
        scratch_shapes=[pltpu.VMEM((n_rows, SLICE_COLS), F32)],
        compiler_params=_cparams("arbitrary"),
        name="s5_sample",
    )(u_s, h0, wb1, a1, cmat, dvec)


def _merge_kernel(z_ref, ya_ref, pa_ref, pb_ref, bga_ref, bgb_ref, wga_ref, wgb_ref, wo_ref, o_ref):
    j = pl.program_id(1)
    z = z_ref[...]
    glu_a = jnp.dot(z, wga_ref[...].astype(BF16), preferred_element_type=F32)
    glu_b = jnp.dot(z, wgb_ref[...].astype(BF16), preferred_element_type=F32)
    yb = glu_a * jax.nn.sigmoid(glu_b)
    g1 = jax.nn.sigmoid(pa_ref[...] + bga_ref[...])
    g2 = jax.nn.sigmoid(pb_ref[...] + bgb_ref[...])
    merged = (g1 * ya_ref[...] + g2 * yb).astype(BF16)
    part = jnp.dot(merged, wo_ref[...].astype(BF16), preferred_element_type=F32)

    @pl.when(j == 0)
    def _():
        o_ref[...] = part

    @pl.when(j > 0)
    def _():
        o_ref[...] += part


def _merge(z, z_index, ya, proj, b_gate, w_glu, w_o, layer, name):
    n_rows = ya.shape[0]
    nj = D_MODEL // TJ_MERGE
    ga0 = (3 * WC + WS) // TJ_MERGE
    gb0 = ga0 + nj
    in_specs = [
        pl.BlockSpec((TM_MERGE, WS), z_index),
        pl.BlockSpec((TM_MERGE, TJ_MERGE), lambda i, j: (i, j)),
        pl.BlockSpec((TM_MERGE, TJ_MERGE), lambda i, j: (i, ga0 + j)),
        pl.BlockSpec((TM_MERGE, TJ_MERGE), lambda i, j: (i, gb0 + j)),
        pl.BlockSpec((None, 1, TJ_MERGE), lambda i, j: (layer, 0, j)),
        pl.BlockSpec((None, 1, TJ_MERGE), lambda i, j: (layer, 0, nj + j)),
        pl.BlockSpec((None, WS, TJ_MERGE), lambda i, j: (layer, 0, j)),
        pl.BlockSpec((None, WS, TJ_MERGE), lambda i, j: (layer, 0, nj + j)),
        pl.BlockSpec((None, TJ_MERGE, D_MODEL), lambda i, j: (layer, j, 0)),
    ]
    return pl.pallas_call(
        _merge_kernel,
        grid=(n_rows // TM_MERGE, nj),
        in_specs=in_specs,
        out_specs=pl.BlockSpec((TM_MERGE, D_MODEL), lambda i, j: (i, 0)),
        out_shape=jax.ShapeDtypeStruct((n_rows, D_MODEL), F32),
        compiler_params=_cparams("arbitrary", "arbitrary"),
        name=name,
    )(z, ya, proj, proj, b_gate, b_gate, w_glu, w_glu, w_o)


def _router_kernel(xp_ref, dp_ref, xs_ref, ds_ref, g_ref, whi_ref, wlo_ref, br_ref,
                   xmid_ref, xn_ref, eid_ref, wts_ref, rank_ref, cnt_ref, run_ref, *, np_blocks):
    i = pl.program_id(0)
    tm = xp_ref.shape[0]

    @pl.when(i == 0)
    def _():
        run_ref[...] = jnp.zeros_like(run_ref)

    @pl.when(i < np_blocks)
    def _():
        xmid_ref[...] = xp_ref[...] + dp_ref[...]

    @pl.when(i >= np_blocks)
    def _():
        xmid_ref[...] = xs_ref[...] + ds_ref[...]

    xm = xmid_ref[...]
    xn = _rmsnorm(xm, g_ref[...])
    xn_ref[...] = xn
    hi = xn.astype(BF16)
    lo = (xn - hi.astype(F32)).astype(BF16)
    nt = (((1,), (1,)), ((), ()))
    whi = whi_ref[...]
    wlo = wlo_ref[...]
    lt = (lax.dot_general(whi, hi, nt, preferred_element_type=F32)
          + lax.dot_general(wlo, hi, nt, preferred_element_type=F32)
          + lax.dot_general(whi, lo, nt, preferred_element_type=F32))
    lt = lt + br_ref[...]

    neg = jnp.float32(-jnp.inf)
    row8 = lax.broadcasted_iota(jnp.int32, (SUBLANES, tm), 0)
    lg = jnp.where(row8 < N_GROUPS, lt[0:SUBLANES], neg)
    gmax = jnp.max(lg, axis=0, keepdims=True)
    gi = jnp.min(jnp.where(lg == gmax, row8, SUBLANES), axis=0, keepdims=True)
    gp = 1.0 / jnp.sum(jnp.exp(lg - gmax), axis=0, keepdims=True)

    sel = lt[SUBLANES:2 * SUBLANES]
    for g in range(1, N_GROUPS):
        sel = jnp.where(gi == g, lt[(g + 1) * SUBLANES:(g + 2) * SUBLANES], sel)
    m1 = jnp.max(sel, axis=0, keepdims=True)
    i1 = jnp.min(jnp.where(sel == m1, row8, SUBLANES), axis=0, keepdims=True)
    sel2 = jnp.where(row8 == i1, neg, sel)
    m2 = jnp.max(sel2, axis=0, keepdims=True)
    i2 = jnp.min(jnp.where(sel2 == m2, row8, SUBLANES), axis=0, keepdims=True)
    e = jnp.exp(m2 - m1)
    w1 = gp / (1.0 + e)
    w2 = gp * e / (1.0 + e)
    e1 = gi * EXPERTS_PER_GROUP + i1
    e2 = gi * EXPERTS_PER_GROUP + i2
    eid_ref[0:1, :] = e1
    eid_ref[1:2, :] = e2
    wts_ref[0:1, :] = w1
    wts_ref[1:2, :] = w2

    erow = lax.broadcasted_iota(jnp.int32, (N_EXPERTS, tm), 0)
    oh1 = jnp.where(erow == e1, 1.0, 0.0)
    oh2 = jnp.where(erow == e2, 1.0, 0.0)
    oh = jnp.concatenate([oh1, oh2], axis=0).astype(BF16)
    tri = jnp.where(lax.broadcasted_iota(jnp.int32, (tm, tm), 0) < lax.broadcasted_iota(jnp.int32, (tm, tm), 1),
                    1.0, 0.0).astype(BF16)
    pref = jnp.dot(oh, tri, preferred_element_type=F32)
    p1 = pref[:N_EXPERTS]
    p2 = pref[N_EXPERTS:]
    tot1 = jnp.sum(oh1, axis=1, keepdims=True)
    tot2 = jnp.sum(oh2, axis=1, keepdims=True)
    run = run_ref[...]
    r1 = jnp.sum(oh1 * (run + p1), axis=0, keepdims=True)
    r2 = jnp.sum(oh2 * (run + tot1 + p2), axis=0, keepdims=True)
    rank_ref[0:1, :] = r1.astype(jnp.int32)
    rank_ref[1:2, :] = r2.astype(jnp.int32)
    run = run + tot1 + tot2
    run_ref[...] = run
    cnt_ref[...] = run


def _router(x_p, d_p, x_s, d_s, norm_g, whi, wlo, br, layer):
    np_blocks = x_p.shape[0] // TM_ROUTE
    t = x_p.shape[0] + x_s.shape[0]
    row = lambda i: (i, 0)
    row_p = lambda i: (jnp.minimum(i, np_blocks - 1), 0)
    row_s = lambda i: (jnp.maximum(i - np_blocks, 0), 0)
    col = lambda i: (0, i)
    fixed = lambda i: (0, 0)
    return pl.pallas_call(
        functools.partial(_router_kernel, np_blocks=np_blocks),
        grid=(t // TM_ROUTE,),
        in_specs=[
            pl.BlockSpec((TM_ROUTE, D_MODEL), row_p),
            pl.BlockSpec((TM_ROUTE, D_MODEL), row_p),
            pl.BlockSpec((TM_ROUTE, D_MODEL), row_s),
            pl.BlockSpec((TM_ROUTE, D_MODEL), row_s),
            pl.BlockSpec((None, 1, D_MODEL), lambda i: (layer, 0, 0)),
            pl.BlockSpec((LANES, D_MODEL), fixed),
            pl.BlockSpec((LANES, D_MODEL), fixed),
            pl.BlockSpec((LANES, 1), fixed),
        ],
        out_specs=[
            pl.BlockSpec((TM_ROUTE, D_MODEL), row),
            pl.BlockSpec((TM_ROUTE, D_MODEL), row),
            pl.BlockSpec((TOP_K, TM_ROUTE), col),
            pl.BlockSpec((TOP_K, TM_ROUTE), col),
            pl.BlockSpec((TOP_K, TM_ROUTE), col),
            pl.BlockSpec((N_EXPERTS, 1), fixed),
        ],
        out_shape=[
            jax.ShapeDtypeStruct((t, D_MODEL), F32),
            jax.ShapeDtypeStruct((t, D_MODEL), F32),
            jax.ShapeDtypeStruct((TOP_K, t), jnp.int32),
            jax.ShapeDtypeStruct((TOP_K, t), F32),
            jax.ShapeDtypeStruct((TOP_K, t), jnp.int32),
            jax.ShapeDtypeStruct((N_EXPERTS, 1), F32),
        ],
        scratch_shapes=[pltpu.VMEM((N_EXPERTS, 1), F32)],
        compiler_params=_cparams("arbitrary"),
        name="moe_router",
    )(x_p, d_p, x_s, d_s, norm_g, whi, wlo, br)


def _row_copy(src_hbm, dst_hbm, src_row, dst_row, sem):
    return pltpu.make_async_copy(src_hbm.at[pl.ds(src_row, 1), :], dst_hbm.at[pl.ds(dst_row, 1), :], sem)


def _dispatch_kernel(pos_ref, xn_hbm, xs_in_hbm, xs_hbm, sems, *, n_tokens):
    del xs_in_hbm
    i = pl.program_id(0)
    n = pl.num_programs(0)
    slot = i % 2
    base = i * TM_DISPATCH

    def issue(r, _):
        t = base + r
        for k in range(TOP_K):
            _row_copy(xn_hbm, xs_hbm, t, pos_ref[k * n_tokens + t], sems.at[slot]).start()
        return 0

    lax.fori_loop(0, TM_DISPATCH, issue, 0, unroll=8)

    def drain(s):
        def wait_one(r, _):
            _row_copy(xn_hbm, xs_hbm, 0, 0, sems.at[s]).wait()
            return 0
        lax.fori_loop(0, TOP_K * TM_DISPATCH, wait_one, 0, unroll=8)

    @pl.when(i > 0)
    def _():
        drain(1 - slot)

    @pl.when(i == n - 1)
    def _():
        drain(slot)


def _dispatch(pos_flat, xn, xs_init):
    t = xn.shape[0]
    return pl.pallas_call(
        functools.partial(_dispatch_kernel, n_tokens=t),
        grid_spec=pltpu.PrefetchScalarGridSpec(
            num_scalar_prefetch=1,
            grid=(t // TM_DISPATCH,),
            in_specs=[pl.BlockSpec(memory_space=pl.ANY), pl.BlockSpec(memory_space=pl.ANY)],
            out_specs=pl.BlockSpec(memory_space=pl.ANY),
            scratch_shapes=[pltpu.SemaphoreType.DMA((2,))],
        ),
        out_shape=jax.ShapeDtypeStruct(xs_init.shape, xs_init.dtype),
        input_output_aliases={2: 0},
        compiler_params=pltpu.CompilerParams(dimension_semantics=("arbitrary",), has_side_effects=True),
        name="moe_dispatch",
    )(pos_flat, xn, xs_init)


def _experts_kernel(te_ref, nu_ref, xs_ref, wg_ref, wu_ref, wd_ref, y_ref, wgs, wus, wds):
    i = pl.program_id(0)

    @pl.when(i < nu_ref[0])
    def _():
        prev = te_ref[jnp.maximum(i - 1, 0)]

        @pl.when((i == 0) | (te_ref[i] != prev))
        def _():
            wgs[...] = wg_ref[...].astype(BF16)
            wus[...] = wu_ref[...].astype(BF16)
            wds[...] = wd_ref[...].astype(BF16)

        x = xs_ref[...].astype(BF16)
        hg = jnp.dot(x, wgs[...], preferred_element_type=F32)
        hu = jnp.dot(x, wus[...], preferred_element_type=F32)
        h = (hg * jax.nn.sigmoid(hg) * hu).astype(BF16)
        y_ref[...] = jnp.dot(h, wds[...], preferred_element_type=F32)

    @pl.when(i >= nu_ref[0])
    def _():
        y_ref[...] = jnp.zeros_like(y_ref)


def _experts(tile_expert, n_used, xs, w_gate, w_up, w_down, layer):
    n_tiles = xs.shape[0] // TE
    row = lambda i, te, nu: (jnp.minimum(i, nu[0] - 1), 0)
    row_out = lambda i, te, nu: (i, 0)
    return pl.pallas_call(
        _experts_kernel,
        grid_spec=pltpu.PrefetchScalarGridSpec(
            num_scalar_prefetch=2,
            grid=(n_tiles,),
            in_specs=[
                pl.BlockSpec((TE, D_MODEL), row),
                pl.BlockSpec((None, None, D_MODEL, D_EXPERT), lambda i, te, nu: (layer, te[i], 0, 0)),
                pl.BlockSpec((None, None, D_MODEL, D_EXPERT), lambda i, te, nu: (layer, te[i], 0, 0)),
                pl.BlockSpec((None, None, D_EXPERT, D_MODEL), lambda i, te, nu: (layer, te[i], 0, 0)),
            ],
            out_specs=pl.BlockSpec((TE, D_MODEL), row_out),
            scratch_shapes=[
                pltpu.VMEM((D_MODEL, D_EXPERT), BF16),
                pltpu.VMEM((D_MODEL, D_EXPERT), BF16),
                pltpu.VMEM((D_EXPERT, D_MODEL), BF16),
            ],
        ),
        out_shape=jax.ShapeDtypeStruct(xs.shape, F32),
        compiler_params=_cparams("arbitrary"),
        name="moe_experts",
    )(tile_expert, n_used, xs, w_gate, w_up, w_down)


def _combine_kernel(pos_ref, x_ref, w_ref, gf_ref, y_hbm, op_ref, os_ref, ybuf, sems,
                    *, n_tokens, final, np_blocks):
    i = pl.program_id(0)
    n = pl.num_programs(0)
    slot = i % 2
    tm = x_ref.shape[0]

    def issue(step, s):
        base = step * tm

        def body(r, _):
            for k in range(TOP_K):
                p = pos_ref[k * n_tokens + base + r]
                pltpu.make_async_copy(y_hbm.at[pl.ds(p, 1), :], ybuf.at[s, k, pl.ds(r, 1), :], sems.at[s]).start()
            return 0

        lax.fori_loop(0, tm, body, 0, unroll=8)

    @pl.when(i == 0)
    def _():
        issue(0, 0)

    @pl.when(i + 1 < n)
    def _():
        issue(i + 1, 1 - slot)

    def wait_one(r, _):
        pltpu.make_async_copy(y_hbm.at[pl.ds(0, 1), :], ybuf.at[slot, 0, pl.ds(0, 1), :], sems.at[slot]).wait()
        return 0

    lax.fori_loop(0, TOP_K * tm, wait_one, 0, unroll=8)

    w = w_ref[...]
    out = x_ref[...] + w[:, 0:1] * ybuf[slot, 0] + w[:, 1:2] * ybuf[slot, 1]
    if final:
        out = _rmsnorm(out, gf_ref[...])

    @pl.when(i < np_blocks)
    def _():
        op_ref[...] = out

    @pl.when(i >= np_blocks)
    def _():
        os_ref[...] = out


def _combine(pos_flat, x_mid, w_t, g_final, y, final, n_prompt):
    t = x_mid.shape[0]
    np_blocks = n_prompt // TM_COMBINE
    row = lambda i, pos: (i, 0)
    out_specs = [
        pl.BlockSpec((TM_COMBINE, D_MODEL), lambda i, pos: (jnp.minimum(i, np_blocks - 1), 0)),
        pl.BlockSpec((TM_COMBINE, D_MODEL), lambda i, pos: (jnp.maximum(i - np_blocks, 0), 0)),
    ]
    out_shape = [
        jax.ShapeDtypeStruct((n_prompt, D_MODEL), F32),
        jax.ShapeDtypeStruct((t - n_prompt, D_MODEL), F32),
    ]
    return pl.pallas_call(
        functools.partial(_combine_kernel, n_tokens=t, final=final, np_blocks=np_blocks),
        grid_spec=pltpu.PrefetchScalarGridSpec(
            num_scalar_prefetch=1,
            grid=(t // TM_COMBINE,),
            in_specs=[
                pl.BlockSpec((TM_COMBINE, D_MODEL), row),
                pl.BlockSpec((TM_COMBINE, TOP_K), row),
                pl.BlockSpec((1, D_MODEL), lambda i, pos: (0, 0)),
                pl.BlockSpec(memory_space=pl.ANY),
            ],
            out_specs=out_specs,
            scratch_shapes=[
                pltpu.VMEM((2, TOP_K, TM_COMBINE, D_MODEL), F32),
                pltpu.SemaphoreType.DMA((2,)),
            ],
        ),
        out_shape=out_shape,
        compiler_params=_cparams("arbitrary"),
        name="moe_combine_final" if final else "moe_combine",
    )(pos_flat, x_mid, w_t, g_final, y)


def _routing_plan(eid, rank, counts, n_tiles):
    cnt = counts[:, 0].astype(jnp.int32)
    padded = ((cnt + TE - 1) // TE) * TE
    ends = jnp.cumsum(padded)
    off = ends - padded
    n_used = (ends[-1] // TE).astype(jnp.int32)
    pos = rank + jnp.take(off, eid)
    tile_start = jnp.arange(n_tiles, dtype=jnp.int32) * TE
    te = jnp.sum((tile_start[:, None] >= ends[None, :]).astype(jnp.int32), axis=1)
    te = jnp.minimum(te, N_EXPERTS - 1)
    te = jnp.take(te, jnp.minimum(jnp.arange(n_tiles, dtype=jnp.int32), n_used - 1))
    return pos.reshape(-1), te, n_used.reshape(1)


def kernel(x_prompt, x_sample, state_conv, state_ssm_re, state_ssm_im, norm_mix, w_in, b_gate, conv_w, w_out_conv, ssm_lambda_re, ssm_lambda_im, ssm_log_step, ssm_b_re, ssm_b_im, ssm_c_re, ssm_c_im, ssm_d, w_glu, w_o, norm_ffn, w_router_group, b_router_group, w_router_expert, b_router_expert, w_exp_gate, w_exp_up, w_exp_down, norm_final):
    nb, seq, dm = x_prompt.shape
    n_seq, n_steps, _ = x_sample.shape
    depth = w_in.shape[0]
    assert dm == D_MODEL and w_in.shape[2] == N_PROJ and w_exp_gate.shape[1:] == (N_EXPERTS, D_MODEL, D_EXPERT)
    assert ssm_lambda_re.shape[1:] == (SSM_G, SSM_P) and ssm_b_re.shape[3] == SSM_H
    assert 2 * nb == SUBLANES and n_seq % SUBLANES == 0
    n_p = nb * seq
    n_s = n_seq * n_steps
    t_all = n_p + n_s
    assert seq % TM_PROJ == 0 and n_s == TM_PROJ and n_s == TM_MERGE and seq % TM_CONV == 0
    assert t_all % TM_ROUTE == 0 and t_all % TM_DISPATCH == 0 and n_p % TM_COMBINE == 0 and n_s % TM_COMBINE == 0
    n_tiles = (TOP_K * t_all + N_EXPERTS * (TE - 1)) // TE

    x_p = x_prompt.reshape(n_p, dm)
    x_s = jnp.transpose(x_sample, (1, 0, 2)).reshape(n_s, dm)
    norm_mix3 = norm_mix.reshape(depth, 1, dm)
    norm_ffn3 = norm_ffn.reshape(depth, 1, dm)
    b_gate3 = b_gate.reshape(depth, 1, 2 * dm)
    zero_conv = jnp.zeros((nb, SUBLANES, WC), F32)
    seq_blks = seq // TM_PROJ

    def u_index_prompt(i, j, u_first):
        return (i % seq_blks, (i // seq_blks) * (WS // TN_PROJ) + jnp.clip(j - u_first, 0, WS // TN_PROJ - 1))

    def u_index_sample(i, j, u_first):
        return (0, jnp.clip(j - u_first, 0, WS // TN_PROJ - 1))

    conv_p, re_p, im_p, conv_s, re_s, im_s = [], [], [], [], [], []
    for l in range(depth):
        lb, a2, bb, ab = _discretize(ssm_lambda_re[l], ssm_lambda_im[l], ssm_log_step[l], ssm_b_re[l], ssm_b_im[l])
        bb = bb.reshape(2, SSM_H, SSM_G, SSM_P)
        ab = ab.reshape(2, SSM_H, SSM_G, SSM_P)
        wb_cur = jnp.concatenate([_block_diag_in(bb[0]), _block_diag_in(bb[1])], axis=2)
        wb_prev = jnp.concatenate([_block_diag_in(ab[0]), _block_diag_in(ab[1])], axis=2)
        wb2 = jnp.concatenate([wb_cur, wb_prev], axis=1).astype(BF16)
        wb1 = wb_cur.astype(BF16)
        a2r = _slice_rows(a2)
        a1r = _slice_rows(lb)
        cmat = jnp.concatenate([_block_diag_out(ssm_c_re[l]), -_block_diag_out(ssm_c_im[l])], axis=1).astype(BF16)
        dvec = ssm_d[l].reshape(N_SLICES, 1, LANES)

        proj_p, u_tm = _in_proj(x_p, norm_mix3, w_in, l, (seq, nb * WS), u_index_prompt, "in_proj_prompt")
        ya_p, cst_p = _conv_branch(proj_p, zero_conv, conv_w, w_out_conv, l, TM_CONV, 1, seq // TM_CONV,
                                   "conv_prompt")
        z_p, hs_p = _ssm_prompt(u_tm.reshape(n_p, WS), wb2, a2r, cmat, dvec, nb)
        d_p = _merge(z_p.reshape(seq, nb * WS), lambda i, j: (i % seq_blks, i // seq_blks),
                     ya_p, proj_p, b_gate3, w_glu, w_o, l, "merge_prompt")

        proj_s, u_s = _in_proj(x_s, norm_mix3, w_in, l, (n_s, WS), u_index_sample, "in_proj_sample")
        cinit = jnp.transpose(state_conv[l], (1, 0, 2)).reshape(1, 2 * n_seq, WC)
        ya_s, cst_s = _conv_branch(proj_s, cinit, conv_w, w_out_conv, l, n_s, n_seq, 1, "conv_sample")
        h0 = jnp.concatenate([state_ssm_re[l].reshape(n_seq, N_SLICES, SLICE_STATES),
                              state_ssm_im[l].reshape(n_seq, N_SLICES, SLICE_STATES)], axis=2)
        z_s, hs_s = _ssm_sample(u_s, h0.reshape(n_seq, N_SLICES * SLICE_COLS), wb1, a1r, cmat, dvec,
                                n_seq, n_steps)
        d_s = _merge(z_s, lambda i, j: (0, 0), ya_s, proj_s, b_gate3, w_glu, w_o, l, "merge_sample")

        conv_p.append(cst_p[:, SUBLANES - 2:, :])
        hs_p = hs_p[:, nb:, :].reshape(N_SLICES, nb, 2, GROUPS_PER_SLICE, SSM_P)
        hs_p = jnp.transpose(hs_p, (2, 1, 0, 3, 4)).reshape(2, nb, SSM_G, SSM_P)
        re_p.append(hs_p[0])
        im_p.append(hs_p[1])
        conv_s.append(jnp.transpose(cst_s.reshape(2, n_seq, WC), (1, 0, 2)))
        hs_s = hs_s.reshape(n_seq, N_SLICES, 2, GROUPS_PER_SLICE, SSM_P)
        hs_s = jnp.transpose(hs_s, (2, 0, 1, 3, 4)).reshape(2, n_seq, SSM_G, SSM_P)
        re_s.append(hs_s[0])
        im_s.append(hs_s[1])

        wr = jnp.zeros((LANES, dm), F32)
        wr = wr.at[0:N_GROUPS].set(w_router_group[l].T)
        wr = wr.at[SUBLANES:SUBLANES + N_EXPERTS].set(w_router_expert[l].T)
        whi = wr.astype(BF16)
        wlo = (wr - whi.astype(F32)).astype(BF16)
        br = jnp.zeros((LANES, 1), F32)
        br = br.at[0:N_GROUPS, 0].set(b_router_group[l])
        br = br.at[SUBLANES:SUBLANES + N_EXPERTS, 0].set(b_router_expert[l])
        x_mid, xn, eid, wts, rank, counts = _router(x_p, d_p, x_s, d_s, norm_ffn3, whi, wlo, br, l)
        pos_flat, tile_expert, n_used = _routing_plan(eid, rank, counts, n_tiles)
        xs = _dispatch(pos_flat, xn, jnp.zeros((n_tiles * TE, dm), F32))
        y = _experts(tile_expert, n_used, xs, w_exp_gate, w_exp_up, w_exp_down, l)
        final = l == depth - 1
        x_p, x_s = _combine(pos_flat, x_mid, jnp.transpose(wts), norm_final.reshape(1, dm), y, final, n_p)

    y_prompt = x_p.reshape(nb, seq, dm)
    y_sample = jnp.transpose(x_s.reshape(n_steps, n_seq, dm), (1, 0, 2))
    return (y_prompt, y_sample,
            jnp.stack(conv_p), jnp.stack(re_p), jnp.stack(im_p),
            jnp.stack(conv_s), jnp.stack(re_s), jnp.stack(im_s))
```

```python
import functools

import jax
import jax.numpy as jnp
from jax import lax
from jax.experimental import pallas as pl
from jax.experimental.pallas import tpu as pltpu

F32 = jnp.float32
BF16 = jnp.bfloat16
EPS = 1e-6

VMEM_LIMIT_BYTES = 56 * 1024 * 1024
SUBLANES = 8
LANES = 128

D_MODEL = 2048
WC = 1024
WS = 1024
SSM_H = 16
SSM_G = 64
SSM_P = 64
N_PROJ = 3 * WC + WS + 2 * D_MODEL
N_GROUPS = 4
EXPERTS_PER_GROUP = 8
N_EXPERTS = N_GROUPS * EXPERTS_PER_GROUP
TOP_K = 2
D_EXPERT = 512

GROUPS_PER_SLICE = LANES // SSM_H
N_SLICES = SSM_G // GROUPS_PER_SLICE
SLICE_STATES = GROUPS_PER_SLICE * SSM_P
SLICE_COLS = 2 * SLICE_STATES

TM_PROJ = 1024
TN_PROJ = 512
TM_CONV = 512
TN_CONV = 512
TM_SSM = 1024
TM_MERGE = 1024
TJ_MERGE = 512
TM_ROUTE = 512
TE = 256
TM_COMBINE = 256


def _cparams(*sem):
    return pltpu.CompilerParams(dimension_semantics=sem, vmem_limit_bytes=VMEM_LIMIT_BYTES)


def _rmsnorm(x, g):
    ms = jnp.mean(x * x, axis=-1, keepdims=True)
    return x * lax.rsqrt(ms + EPS) * g


def _in_proj_kernel(x_ref, g_ref, w_ref, p_ref, u_ref, xn_ref, *, u_first, u_last):
    j = pl.program_id(1)

    @pl.when(j == 0)
    def _():
        xn_ref[...] = _rmsnorm(x_ref[...], g_ref[...]).astype(BF16)

    acc = jnp.dot(xn_ref[...], w_ref[...].astype(BF16), preferred_element_type=F32)
    p_ref[...] = acc

    @pl.when((j >= u_first) & (j <= u_last))
    def _():
        u_ref[...] = acc


def _in_proj(x, norm_g, w_in, layer, u_shape, u_index, name):
    n_row_blks = x.shape[0] // TM_PROJ
    u_first = (3 * WC) // TN_PROJ
    u_last = (3 * WC + WS) // TN_PROJ - 1
    kern = functools.partial(_in_proj_kernel, u_first=u_first, u_last=u_last)
    return pl.pallas_call(
        kern,
        grid=(n_row_blks, N_PROJ // TN_PROJ),
        in_specs=[
            pl.BlockSpec((TM_PROJ, D_MODEL), lambda i, j: (i, 0)),
            pl.BlockSpec((None, 1, D_MODEL), lambda i, j: (layer, 0, 0)),
            pl.BlockSpec((None, D_MODEL, TN_PROJ), lambda i, j: (layer, 0, j)),
        ],
        out_specs=[
            pl.BlockSpec((TM_PROJ, TN_PROJ), lambda i, j: (i, j)),
            pl.BlockSpec((TM_PROJ, TN_PROJ), functools.partial(u_index, u_first=u_first)),
        ],
        out_shape=[
            jax.ShapeDtypeStruct((n_row_blks * TM_PROJ, N_PROJ), F32),
            jax.ShapeDtypeStruct(u_shape, F32),
        ],
        scratch_shapes=[pltpu.VMEM((TM_PROJ, D_MODEL), BF16)],
        compiler_params=_cparams("arbitrary", "arbitrary"),
        name=name,
    )(x, norm_g, w_in)


def _conv_kernel(b_ref, c_ref, h_ref, init_ref, cw_ref, w_ref, ya_ref, st_ref, bc_ref, carry_ref,
                 *, shift, tiles_per_seq):
    i = pl.program_id(0)
    j = pl.program_id(1)
    tm = b_ref.shape[0]
    nc = carry_ref.shape[0]

    @pl.when(j == 0)
    def _():
        @pl.when(i % tiles_per_seq == 0)
        def _():
            carry_ref[...] = init_ref[...]

        v = c_ref[...] * h_ref[...]
        ext = jnp.concatenate([carry_ref[...], v], axis=0)
        v1 = ext[nc - shift:nc - shift + tm]
        v2 = ext[nc - 2 * shift:nc - 2 * shift + tm]
        cw = cw_ref[...]
        conv = cw[0:1] * v2 + cw[1:2] * v1 + cw[2:3] * v
        bc_ref[...] = (b_ref[...] * conv).astype(BF16)
        carry_ref[...] = v[tm - nc:]
        st_ref[...] = v[tm - nc:]

    ya_ref[...] = jnp.dot(bc_ref[...], w_ref[...].astype(BF16), preferred_element_type=F32)


def _conv_branch(proj, init, conv_w, w_out_conv, layer, tm, shift, tiles_per_seq, name):
    n_rows = proj.shape[0]
    nc = init.shape[1]
    n_seq = init.shape[0]
    kern = functools.partial(_conv_kernel, shift=shift, tiles_per_seq=tiles_per_seq)
    return pl.pallas_call(
        kern,
        grid=(n_rows // tm, D_MODEL // TN_CONV),
        in_specs=[
            pl.BlockSpec((tm, WC), lambda i, j: (i, 0)),
            pl.BlockSpec((tm, WC), lambda i, j: (i, 1)),
            pl.BlockSpec((tm, WC), lambda i, j: (i, 2)),
            pl.BlockSpec((None, nc, WC), lambda i, j: (i // tiles_per_seq, 0, 0)),
            pl.BlockSpec((None, 3, WC), lambda i, j: (layer, 0, 0)),
            pl.BlockSpec((None, WC, TN_CONV), lambda i, j: (layer, 0, j)),
        ],
        out_specs=[
            pl.BlockSpec((tm, TN_CONV), lambda i, j: (i, j)),
            pl.BlockSpec((None, nc, WC), lambda i, j: (i // tiles_per_seq, 0, 0)),
        ],
        out_shape=[
            jax.ShapeDtypeStruct((n_rows, D_MODEL), F32),
            jax.ShapeDtypeStruct((n_seq, nc, WC), F32),
        ],
        scratch_shapes=[pltpu.VMEM((tm, WC), BF16), pltpu.VMEM((nc, WC), F32)],
        compiler_params=_cparams("arbitrary", "arbitrary"),
        name=name,
    )(proj, proj, proj, init, conv_w, w_out_conv)


def _disc_kernel(lr_ref, li_ref, ls_ref, br_ref, bi_ref, lb_ref, a2_ref, bb_ref, ab_ref):
    lr = lr_ref[...]
    li = li_ref[...]
    dt = jnp.exp(ls_ref[...])
    mag = jnp.exp(lr * dt)
    lbr = mag * jnp.cos(li * dt)
    lbi = mag * jnp.sin(li * dt)
    nr, ni = lbr - 1.0, lbi
    den = lr * lr + li * li
    fr = (nr * lr + ni * li) / den
    fi = (ni * lr - nr * li) / den
    lb_ref[0] = lbr
    lb_ref[1] = lbi
    a2_ref[0] = lbr * lbr - lbi * lbi
    a2_ref[1] = 2.0 * lbr * lbi
    frt = jnp.concatenate([fr] * SSM_H, axis=0)
    fit = jnp.concatenate([fi] * SSM_H, axis=0)
    lbrt = jnp.concatenate([lbr] * SSM_H, axis=0)
    lbit = jnp.concatenate([lbi] * SSM_H, axis=0)
    br = br_ref[...]
    bi = bi_ref[...]
    bbr = frt * br - fit * bi
    bbi = frt * bi + fit * br
    bb_ref[0] = bbr
    bb_ref[1] = bbi
    ab_ref[0] = lbrt * bbr - lbit * bbi
    ab_ref[1] = lbrt * bbi + lbit * bbr


def _discretize(lam_re, lam_im, log_step, b_re, b_im):
    g, p = lam_re.shape
    hg = SSM_H * g
    brt = jnp.transpose(b_re, (2, 0, 1)).reshape(hg, p)
    bit = jnp.transpose(b_im, (2, 0, 1)).reshape(hg, p)
    return pl.pallas_call(
        _disc_kernel,
        out_shape=[
            jax.ShapeDtypeStruct((2, g, p), F32),
            jax.ShapeDtypeStruct((2, g, p), F32),
            jax.ShapeDtypeStruct((2, hg, p), F32),
            jax.ShapeDtypeStruct((2, hg, p), F32),
        ],
        name="s5_discretize",
    )(lam_re, lam_im, log_step.reshape(g, 1), brt, bit)


def _block_diag_in(m):
    m = m.reshape(SSM_H, N_SLICES, GROUPS_PER_SLICE, SSM_P)
    eye = jnp.eye(GROUPS_PER_SLICE, dtype=m.dtype)
    out = jnp.einsum('hkgp,gq->kghqp', m, eye)
    return out.reshape(N_SLICES, LANES, SLICE_STATES)


def _block_diag_out(c):
    c = c.reshape(N_SLICES, GROUPS_PER_SLICE, SSM_H, SSM_P)
    eye = jnp.eye(GROUPS_PER_SLICE, dtype=c.dtype)
    out = jnp.einsum('kghp,gq->kgpqh', c, eye)
    return out.reshape(N_SLICES, SLICE_STATES, LANES)


def _slice_rows(a):
    a = a.reshape(2, N_SLICES, SLICE_STATES)
    a = jnp.transpose(a, (1, 0, 2)).reshape(N_SLICES, 1, SLICE_COLS)
    return jnp.broadcast_to(a, (N_SLICES, SUBLANES, SLICE_COLS))


def _gelu_out(x_state, u, c_ref, d_ref):
    y = jnp.dot(x_state.astype(BF16), c_ref[...], preferred_element_type=F32)
    s = y + d_ref[...] * u
    return jax.nn.gelu(s).astype(BF16)


def _ssm_prompt_kernel(u_ref, wb_ref, a_ref, c_ref, d_ref, z_ref, hs_ref, v_ref, st_ref, up_ref, *, nb):
    tb = pl.program_id(1)
    tm = u_ref.shape[0]
    half = SLICE_STATES

    @pl.when(tb == 0)
    def _():
        st_ref[...] = jnp.zeros_like(st_ref)
        up_ref[...] = jnp.zeros_like(up_ref)

    u = u_ref[...]
    ext = jnp.concatenate([up_ref[...], u], axis=0)
    u_prev = ext[SUBLANES - nb:SUBLANES - nb + tm]
    lhs = jnp.concatenate([u, u_prev], axis=1).astype(BF16)
    v_ref[...] = jnp.dot(lhs, wb_ref[...], preferred_element_type=F32)
    up_ref[...] = u[tm - SUBLANES:]

    a_re = a_ref[:, :half]
    a_im = a_ref[:, half:]

    def body(s, carry):
        sre, sim = carry
        r0 = pl.multiple_of(s * SUBLANES, SUBLANES)
        vre = v_ref[pl.ds(r0, SUBLANES), :half]
        vim = v_ref[pl.ds(r0, SUBLANES), half:]
        nre = a_re * sre - a_im * sim + vre
        nim = a_re * sim + a_im * sre + vim
        v_ref[pl.ds(r0, SUBLANES), :half] = nre
        v_ref[pl.ds(r0, SUBLANES), half:] = nim
        return nre, nim

    sre, sim = lax.fori_loop(0, tm // SUBLANES, body, (st_ref[:, :half], st_ref[:, half:]))
    st_ref[:, :half] = sre
    st_ref[:, half:] = sim
    hs_ref[:, :half] = sre
    hs_ref[:, half:] = sim
    z_ref[...] = _gelu_out(v_ref[...], u, c_ref, d_ref)


def _ssm_prompt(u_tm, wb, a2, cmat, dvec, nb):
    n_rows = u_tm.shape[0]
    kern = functools.partial(_ssm_prompt_kernel, nb=nb)
    return pl.pallas_call(
        kern,
        grid=(N_SLICES, n_rows // TM_SSM),
        in_specs=[
            pl.BlockSpec((TM_SSM, LANES), lambda k, t: (t, k)),
            pl.BlockSpec((None, 2 * LANES, SLICE_COLS), lambda k, t: (k, 0, 0)),
            pl.BlockSpec((None, SUBLANES, SLICE_COLS), lambda k, t: (k, 0, 0)),
            pl.BlockSpec((None, SLICE_COLS, LANES), lambda k, t: (k, 0, 0)),
            pl.BlockSpec((None, 1, LANES), lambda k, t: (k, 0, 0)),
        ],
        out_specs=[
            pl.BlockSpec((TM_SSM, LANES), lambda k, t: (t, k)),
            pl.BlockSpec((None, SUBLANES, SLICE_COLS), lambda k, t: (k, 0, 0)),
        ],
        out_shape=[
            jax.ShapeDtypeStruct((n_rows, WS), BF16),
            jax.ShapeDtypeStruct((N_SLICES, SUBLANES, SLICE_COLS), F32),
        ],
        scratch_shapes=[
            pltpu.VMEM((TM_SSM, SLICE_COLS), F32),
            pltpu.VMEM((SUBLANES, SLICE_COLS), F32),
            pltpu.VMEM((SUBLANES, LANES), F32),
        ],
        compiler_params=_cparams("arbitrary", "arbitrary"),
        name="s5_prompt",
    )(u_tm, wb, a2, cmat, dvec)


def _ssm_sample_kernel(u_ref, h0_ref, wb_ref, a_ref, c_ref, d_ref, z_ref, hs_ref, v_ref, *, n_seq, n_steps):
    half = SLICE_STATES
    u = u_ref[...]
    v_ref[...] = jnp.dot(u.astype(BF16), wb_ref[...], preferred_element_type=F32)
    a_re = a_ref[:, :half]
    a_im = a_ref[:, half:]

    def body(q, _):
        n0 = pl.multiple_of(q * SUBLANES, SUBLANES)
        sre = h0_ref[pl.ds(n0, SUBLANES), :half]
        sim = h0_ref[pl.ds(n0, SUBLANES), half:]
        for t in range(n_steps):
            r0 = pl.multiple_of(t * n_seq + n0, SUBLANES)
            vre = v_ref[pl.ds(r0, SUBLANES), :half]
            vim = v_ref[pl.ds(r0, SUBLANES), half:]
            nre = a_re * sre - a_im * sim + vre
            nim = a_re * sim + a_im * sre + vim
            v_ref[pl.ds(r0, SUBLANES), :half] = nre
            v_ref[pl.ds(r0, SUBLANES), half:] = nim
            sre, sim = nre, nim
        hs_ref[pl.ds(n0, SUBLANES), :half] = sre
        hs_ref[pl.ds(n0, SUBLANES), half:] = sim
        return 0

    lax.fori_loop(0, n_seq // SUBLANES, body, 0)
    z_ref[...] = _gelu_out(v_ref[...], u, c_ref, d_ref)


def _ssm_sample(u_s, h0, wb1, a1, cmat, dvec, n_seq, n_steps):
    n_rows = u_s.shape[0]
    kern = functools.partial(_ssm_sample_kernel, n_seq=n_seq, n_steps=n_steps)
    return pl.pallas_call(
        kern,
        grid=(N_SLICES,),
        in_specs=[
            pl.BlockSpec((n_rows, LANES), lambda k: (0, k)),
            pl.BlockSpec((n_seq, SLICE_COLS), lambda k: (0, k)),
            pl.BlockSpec((None, LANES, SLICE_COLS), lambda k: (k, 0, 0)),
            pl.BlockSpec((None, SUBLANES, SLICE_COLS), lambda k: (k, 0, 0)),
            pl.BlockSpec((None, SLICE_COLS, LANES), lambda k: (k, 0, 0)),
            pl.BlockSpec((None, 1, LANES), lambda k: (k, 0, 0)),
        ],
        out_specs=[
            pl.BlockSpec((n_rows, LANES), lambda k: (0, k)),
            pl.BlockSpec((n_seq, SLICE_COLS), lambda k: (0, k)),
        ],
        out_shape=[
            jax.ShapeDtypeStruct((n_rows, WS), BF16),
            jax.ShapeDtypeStruct((n_seq, N_SLICES * SLICE_COLS), F32),
        ],
        scratch_shapes=[pltpu.VMEM((n_rows, SLICE_COLS), F32)],
        compiler_params=_cparams("arbitrary"),
        name="s5_sample",
    )(u_s, h0, wb1, a1, cmat, dvec)


def _merge_kernel(z_ref, ya_ref, pa_ref, pb_ref, bga_ref, bgb_ref, wga_ref, wgb_ref, wo_ref, o_ref):
    j = pl.program_id(1)
    z = z_ref[...]
    glu_a = jnp.dot(z, wga_ref[...].astype(BF16), preferred_element_type=F32)
    glu_b = jnp.dot(z, wgb_ref[...].astype(BF16), preferred_element_type=F32)
    yb = glu_a * jax.nn.sigmoid(glu_b)
    g1 = jax.nn.sigmoid(pa_ref[...] + bga_ref[...])
    g2 = jax.nn.sigmoid(pb_ref[...] + bgb_ref[...])
    merged = (g1 * ya_ref[...] + g2 * yb).astype(BF16)
    part = jnp.dot(merged, wo_ref[...].astype(BF16), preferred_element_type=F32)

    @pl.when(j == 0)
    def _():
        o_ref[...] = part

    @pl.when(j > 0)
    def _():
        o_ref[...] += part


def _merge(z, z_index, ya, proj, b_gate, w_glu, w_o, layer, name):
    n_rows = ya.shape[0]
    nj = D_MODEL // TJ_MERGE
    ga0 = (3 * WC + WS) // TJ_MERGE
    gb0 = ga0 + nj
    in_specs = [
        pl.BlockSpec((TM_MERGE, WS), z_index),
        pl.BlockSpec((TM_MERGE, TJ_MERGE), lambda i, j: (i, j)),
        pl.BlockSpec((TM_MERGE, TJ_MERGE), lambda i, j: (i, ga0 + j)),
        pl.BlockSpec((TM_MERGE, TJ_MERGE), lambda i, j: (i, gb0 + j)),
        pl.BlockSpec((None, 1, TJ_MERGE), lambda i, j: (layer, 0, j)),
        pl.BlockSpec((None, 1, TJ_MERGE), lambda i, j: (layer, 0, nj + j)),
        pl.BlockSpec((None, WS, TJ_MERGE), lambda i, j: (layer, 0, j)),
        pl.BlockSpec((None, WS, TJ_MERGE), lambda i, j: (layer, 0, nj + j)),
        pl.BlockSpec((None, TJ_MERGE, D_MODEL), lambda i, j: (layer, j, 0)),
    ]
    return pl.pallas_call(
        _merge_kernel,
        grid=(n_rows // TM_MERGE, nj),
        in_specs=in_specs,
        out_specs=pl.BlockSpec((TM_MERGE, D_MODEL), lambda i, j: (i, 0)),
        out_shape=jax.ShapeDtypeStruct((n_rows, D_MODEL), F32),
        compiler_params=_cparams("arbitrary", "arbitrary"),
        name=name,
    )(z, ya, proj, proj, b_gate, b_gate, w_glu, w_glu, w_o)


def _router_kernel(xp_ref, dp_ref, xs_ref, ds_ref, g_ref, whi_ref, wlo_ref, br_ref,
                   xmid_ref, xn_ref, eid_ref, wts_ref, rank_ref, cnt_ref, run_ref, *, np_blocks):
    i = pl.program_id(0)
    tm = xp_ref.shape[0]

    @pl.when(i == 0)
    def _():
        run_ref[...] = jnp.zeros_like(run_ref)

    @pl.when(i < np_blocks)
    def _():
        xmid_ref[...] = xp_ref[...] + dp_ref[...]

    @pl.when(i >= np_blocks)
    def _():
        xmid_ref[...] = xs_ref[...] + ds_ref[...]

    xm = xmid_ref[...]
    xn = _rmsnorm(xm, g_ref[...])
    xn_ref[...] = xn
    hi = xn.astype(BF16)
    lo = (xn - hi.astype(F32)).astype(BF16)
    nt = (((1,), (1,)), ((), ()))
    whi = whi_ref[...]
    wlo = wlo_ref[...]
    lt = (lax.dot_general(whi, hi, nt, preferred_element_type=F32)
          + lax.dot_general(wlo, hi, nt, preferred_element_type=F32)
          + lax.dot_general(whi, lo, nt, preferred_element_type=F32))
    lt = lt + br_ref[...]

    neg = jnp.float32(-jnp.inf)
    row8 = lax.broadcasted_iota(jnp.int32, (SUBLANES, tm), 0)
    lg = jnp.where(row8 < N_GROUPS, lt[0:SUBLANES], neg)
    gmax = jnp.max(lg, axis=0, keepdims=True)
    gi = jnp.min(jnp.where(lg == gmax, row8, SUBLANES), axis=0, keepdims=True)
    gp = 1.0 / jnp.sum(jnp.exp(lg - gmax), axis=0, keepdims=True)

    sel = lt[SUBLANES:2 * SUBLANES]
    for g in range(1, N_GROUPS):
        sel = jnp.where(gi == g, lt[(g + 1) * SUBLANES:(g + 2) * SUBLANES], sel)
    m1 = jnp.max(sel, axis=0, keepdims=True)
    i1 = jnp.min(jnp.where(sel == m1, row8, SUBLANES), axis=0, keepdims=True)
    sel2 = jnp.where(row8 == i1, neg, sel)
    m2 = jnp.max(sel2, axis=0, keepdims=True)
    i2 = jnp.min(jnp.where(sel2 == m2, row8, SUBLANES), axis=0, keepdims=True)
    e = jnp.exp(m2 - m1)
    w1 = gp / (1.0 + e)
    w2 = gp * e / (1.0 + e)
    e1 = gi * EXPERTS_PER_GROUP + i1
    e2 = gi * EXPERTS_PER_GROUP + i2
    eid_ref[0:1, :] = e1
    eid_ref[1:2, :] = e2
    wts_ref[0:1, :] = w1
    wts_ref[1:2, :] = w2

    erow = lax.broadcasted_iota(jnp.int32, (N_EXPERTS, tm), 0)
    oh1 = jnp.where(erow == e1, 1.0, 0.0)
    oh2 = jnp.where(erow == e2, 1.0, 0.0)
    oh = jnp.concatenate([oh1, oh2], axis=0).astype(BF16)
    tri = jnp.where(lax.broadcasted_iota(jnp.int32, (tm, tm), 0) < lax.broadcasted_iota(jnp.int32, (tm, tm), 1),
                    1.0, 0.0).astype(BF16)
    pref = jnp.dot(oh, tri, preferred_element_type=F32)
    p1 = pref[:N_EXPERTS]
    p2 = pref[N_EXPERTS:]
    tot1 = jnp.sum(oh1, axis=1, keepdims=True)
    tot2 = jnp.sum(oh2, axis=1, keepdims=True)
    run = run_ref[...]
    r1 = jnp.sum(oh1 * (run + p1), axis=0, keepdims=True)
    r2 = jnp.sum(oh2 * (run + tot1 + p2), axis=0, keepdims=True)
    rank_ref[0:1, :] = r1.astype(jnp.int32)
    rank_ref[1:2, :] = r2.astype(jnp.int32)
    run = run + tot1 + tot2
    run_ref[...] = run
    cnt_ref[...] = run


def _router(x_p, d_p, x_s, d_s, norm_g, whi, wlo, br, layer):
    np_blocks = x_p.shape[0] // TM_ROUTE
    t = x_p.shape[0] + x_s.shape[0]
    row = lambda i: (i, 0)
    row_p = lambda i: (jnp.minimum(i, np_blocks - 1), 0)
    row_s = lambda i: (jnp.maximum(i - np_blocks, 0), 0)
    col = lambda i: (0, i)
    fixed = lambda i: (0, 0)
    return pl.pallas_call(
        functools.partial(_router_kernel, np_blocks=np_blocks),
        grid=(t // TM_ROUTE,),
        in_specs=[
            pl.BlockSpec((TM_ROUTE, D_MODEL), row_p),
            pl.BlockSpec((TM_ROUTE, D_MODEL), row_p),
            pl.BlockSpec((TM_ROUTE, D_MODEL), row_s),
            pl.BlockSpec((TM_ROUTE, D_MODEL), row_s),
            pl.BlockSpec((None, 1, D_MODEL), lambda i: (layer, 0, 0)),
            pl.BlockSpec((LANES, D_MODEL), fixed),
            pl.BlockSpec((LANES, D_MODEL), fixed),
            pl.BlockSpec((LANES, 1), fixed),
        ],
        out_specs=[
            pl.BlockSpec((TM_ROUTE, D_MODEL), row),
            pl.BlockSpec((TM_ROUTE, D_MODEL), row),
            pl.BlockSpec((TOP_K, TM_ROUTE), col),
            pl.BlockSpec((TOP_K, TM_ROUTE), col),
            pl.BlockSpec((TOP_K, TM_ROUTE), col),
            pl.BlockSpec((N_EXPERTS, 1), fixed),
        ],
        out_shape=[
            jax.ShapeDtypeStruct((t, D_MODEL), F32),
            jax.ShapeDtypeStruct((t, D_MODEL), F32),
            jax.ShapeDtypeStruct((TOP_K, t), jnp.int32),
            jax.ShapeDtypeStruct((TOP_K, t), F32),
            jax.ShapeDtypeStruct((TOP_K, t), jnp.int32),
            jax.ShapeDtypeStruct((N_EXPERTS, 1), F32),
        ],
        scratch_shapes=[pltpu.VMEM((N_EXPERTS, 1), F32)],
        compiler_params=_cparams("arbitrary"),
        name="moe_router",
    )(x_p, d_p, x_s, d_s, norm_g, whi, wlo, br)


def _invert_kernel(pos_ref, sorted_ref, *, n_tokens):
    def clear(p, _):
        sorted_ref[p] = 0
        return 0

    lax.fori_loop(0, sorted_ref.shape[0], clear, 0, unroll=8)

    def place(t, _):
        for k in range(TOP_K):
            sorted_ref[pos_ref[k * n_tokens + t]] = t
        return 0

    lax.fori_loop(0, n_tokens, place, 0, unroll=8)


def _invert(pos_flat, n_tokens, n_sorted):
    return pl.pallas_call(
        functools.partial(_invert_kernel, n_tokens=n_tokens),
        in_specs=[pl.BlockSpec(memory_space=pltpu.SMEM)],
        out_specs=pl.BlockSpec(memory_space=pltpu.SMEM),
        out_shape=jax.ShapeDtypeStruct((n_sorted,), jnp.int32),
        name="moe_invert",
    )(pos_flat)


def _experts_kernel(te_ref, nu_ref, st_ref, xn_hbm, wg_ref, wu_ref, wd_ref, y_ref, wgs, wus, wds, xbuf, sems):
    i = pl.program_id(0)
    slot = i % 2

    def gather(tile, s):
        base = tile * TE

        def body(r, _):
            pltpu.make_async_copy(xn_hbm.at[pl.ds(st_ref[base + r], 1), :],
                                  xbuf.at[s, pl.ds(r, 1), :], sems.at[s]).start()
            return 0

        lax.fori_loop(0, TE, body, 0, unroll=8)

    @pl.when(i == 0)
    def _():
        gather(0, 0)

    @pl.when(i + 1 < nu_ref[0])
    def _():
        gather(i + 1, 1 - slot)

    @pl.when(i < nu_ref[0])
    def _():
        pltpu.make_async_copy(xn_hbm.at[pl.ds(0, TE), :], xbuf.at[slot], sems.at[slot]).wait()
        prev = te_ref[jnp.maximum(i - 1, 0)]

        @pl.when((i == 0) | (te_ref[i] != prev))
        def _():
            wgs[...] = wg_ref[...].astype(BF16)
            wus[...] = wu_ref[...].astype(BF16)
            wds[...] = wd_ref[...].astype(BF16)

        x = xbuf[slot].astype(BF16)
        hg = jnp.dot(x, wgs[...], preferred_element_type=F32)
        hu = jnp.dot(x, wus[...], preferred_element_type=F32)
        h = (hg * jax.nn.sigmoid(hg) * hu).astype(BF16)
        y_ref[...] = jnp.dot(h, wds[...], preferred_element_type=F32)

    @pl.when(i >= nu_ref[0])
    def _():
        y_ref[...] = jnp.zeros_like(y_ref)


def _experts(tile_expert, n_used, sorted_token, xn, w_gate, w_up, w_down, layer):
    n_tiles = sorted_token.shape[0] // TE
    wmap = lambda i, te, nu, st: (layer, te[i], 0, 0)
    return pl.pallas_call(
        _experts_kernel,
        grid_spec=pltpu.PrefetchScalarGridSpec(
            num_scalar_prefetch=3,
            grid=(n_tiles,),
            in_specs=[
                pl.BlockSpec(memory_space=pl.ANY),
                pl.BlockSpec((None, None, D_MODEL, D_EXPERT), wmap),
                pl.BlockSpec((None, None, D_MODEL, D_EXPERT), wmap),
                pl.BlockSpec((None, None, D_EXPERT, D_MODEL), wmap),
            ],
            out_specs=pl.BlockSpec((TE, D_MODEL), lambda i, te, nu, st: (i, 0)),
            scratch_shapes=[
                pltpu.VMEM((D_MODEL, D_EXPERT), BF16),
                pltpu.VMEM((D_MODEL, D_EXPERT), BF16),
                pltpu.VMEM((D_EXPERT, D_MODEL), BF16),
                pltpu.VMEM((2, TE, D_MODEL), F32),
                pltpu.SemaphoreType.DMA((2,)),
            ],
        ),
        out_shape=jax.ShapeDtypeStruct((n_tiles * TE, D_MODEL), F32),
        compiler_params=_cparams("arbitrary"),
        name="moe_experts",
    )(tile_expert, n_used, sorted_token, xn, w_gate, w_up, w_down)


def _combine_kernel(pos_ref, x_ref, w_ref, gf_ref, y_hbm, op_ref, os_ref, ybuf, sems,
                    *, n_tokens, final, np_blocks):
    i = pl.program_id(0)
    n = pl.num_programs(0)
    slot = i % 2
    tm = x_ref.shape[0]

    def issue(step, s):
        base = step * tm

        def body(r, _):
            for k in range(TOP_K):
                p = pos_ref[k * n_tokens + base + r]
                pltpu.make_async_copy(y_hbm.at[pl.ds(p, 1), :], ybuf.at[s, k, pl.ds(r, 1), :], sems.at[s]).start()
            return 0

        lax.fori_loop(0, tm, body, 0, unroll=8)

    @pl.when(i == 0)
    def _():
        issue(0, 0)

    @pl.when(i + 1 < n)
    def _():
        issue(i + 1, 1 - slot)

    for k in range(TOP_K):
        pltpu.make_async_copy(y_hbm.at[pl.ds(0, tm), :], ybuf.at[slot, k], sems.at[slot]).wait()

    w = w_ref[...]
    out = x_ref[...] + w[:, 0:1] * ybuf[slot, 0] + w[:, 1:2] * ybuf[slot, 1]
    if final:
        out = _rmsnorm(out, gf_ref[...])

    @pl.when(i < np_blocks)
    def _():
        op_ref[...] = out

    @pl.when(i >= np_blocks)
    def _():
        os_ref[...] = out


def _combine(pos_flat, x_mid, w_t, g_final, y, final, n_prompt):
    t = x_mid.shape[0]
    np_blocks = n_prompt // TM_COMBINE
    row = lambda i, pos: (i, 0)
    out_specs = [
        pl.BlockSpec((TM_COMBINE, D_MODEL), lambda i, pos: (jnp.minimum(i, np_blocks - 1), 0)),
        pl.BlockSpec((TM_COMBINE, D_MODEL), lambda i, pos: (jnp.maximum(i - np_blocks, 0), 0)),
    ]
    out_shape = [
        jax.ShapeDtypeStruct((n_prompt, D_MODEL), F32),
        jax.ShapeDtypeStruct((t - n_prompt, D_MODEL), F32),
    ]
    return pl.pallas_call(
        functools.partial(_combine_kernel, n_tokens=t, final=final, np_blocks=np_blocks),
        grid_spec=pltpu.PrefetchScalarGridSpec(
            num_scalar_prefetch=1,
            grid=(t // TM_COMBINE,),
            in_specs=[
                pl.BlockSpec((TM_COMBINE, D_MODEL), row),
                pl.BlockSpec((TM_COMBINE, TOP_K), row),
                pl.BlockSpec((1, D_MODEL), lambda i, pos: (0, 0)),
                pl.BlockSpec(memory_space=pl.ANY),
            ],
            out_specs=out_specs,
            scratch_shapes=[
                pltpu.VMEM((2, TOP_K, TM_COMBINE, D_MODEL), F32),
                pltpu.SemaphoreType.DMA((2,)),
            ],
        ),
        out_shape=out_shape,
        compiler_params=_cparams("arbitrary"),
        name="moe_combine_final" if final else "moe_combine",
    )(pos_flat, x_mid, w_t, g_final, y)


def _routing_plan(eid, rank, counts, n_tiles):
    cnt = counts[:, 0].astype(jnp.int32)
    padded = ((cnt + TE - 1) // TE) * TE
    ends = jnp.cumsum(padded)
    off = ends - padded
    n_used = (ends[-1] // TE).astype(jnp.int32)
    experts = jnp.arange(N_EXPERTS, dtype=jnp.int32)
    pos = rank + jnp.sum(jnp.where(eid[..., None] == experts, off, 0), axis=-1)
    tile_start = jnp.arange(n_tiles, dtype=jnp.int32) * TE
    te = jnp.sum((tile_start[:, None] >= ends[None, :]).astype(jnp.int32), axis=1)
    last_expert = jnp.max(jnp.where(cnt > 0, experts, 0))
    te = jnp.where(tile_start < ends[-1], te, last_expert)
    return pos.reshape(-1), te, n_used.reshape(1)


def kernel(x_prompt, x_sample, state_conv, state_ssm_re, state_ssm_im, norm_mix, w_in, b_gate, conv_w, w_out_conv, ssm_lambda_re, ssm_lambda_im, ssm_log_step, ssm_b_re, ssm_b_im, ssm_c_re, ssm_c_im, ssm_d, w_glu, w_o, norm_ffn, w_router_group, b_router_group, w_router_expert, b_router_expert, w_exp_gate, w_exp_up, w_exp_down, norm_final):
    nb, seq, dm = x_prompt.shape
    n_seq, n_steps, _ = x_sample.shape
    depth = w_in.shape[0]
    assert dm == D_MODEL and w_in.shape[2] == N_PROJ and w_exp_gate.shape[1:] == (N_EXPERTS, D_MODEL, D_EXPERT)
    assert ssm_lambda_re.shape[1:] == (SSM_G, SSM_P) and ssm_b_re.shape[3] == SSM_H
    assert 2 * nb == SUBLANES and n_seq % SUBLANES == 0
    n_p = nb * seq
    n_s = n_seq * n_steps
    t_all = n_p + n_s
    assert seq % TM_PROJ == 0 and n_s == TM_PROJ and n_s == TM_MERGE and seq % TM_CONV == 0
    assert n_p % TM_ROUTE == 0 and n_s % TM_ROUTE == 0 and n_p % TM_COMBINE == 0 and n_s % TM_COMBINE == 0
    n_tiles = (TOP_K * t_all + N_EXPERTS * (TE - 1)) // TE

    x_p = x_prompt.reshape(n_p, dm)
    x_s = jnp.transpose(x_sample, (1, 0, 2)).reshape(n_s, dm)
    norm_mix3 = norm_mix.reshape(depth, 1, dm)
    norm_ffn3 = norm_ffn.reshape(depth, 1, dm)
    b_gate3 = b_gate.reshape(depth, 1, 2 * dm)
    zero_conv = jnp.zeros((nb, SUBLANES, WC), F32)
    seq_blks = seq // TM_PROJ

    def u_index_prompt(i, j, u_first):
        return (i % seq_blks, (i // seq_blks) * (WS // TN_PROJ) + jnp.clip(j - u_first, 0, WS // TN_PROJ - 1))

    def u_index_sample(i, j, u_first):
        return (0, jnp.clip(j - u_first, 0, WS // TN_PROJ - 1))

    conv_p, re_p, im_p, conv_s, re_s, im_s = [], [], [], [], [], []
    for l in range(depth):
        lb, a2, bb, ab = _discretize(ssm_lambda_re[l], ssm_lambda_im[l], ssm_log_step[l], ssm_b_re[l], ssm_b_im[l])
        bb = bb.reshape(2, SSM_H, SSM_G, SSM_P)
        ab = ab.reshape(2, SSM_H, SSM_G, SSM_P)
        wb_cur = jnp.concatenate([_block_diag_in(bb[0]), _block_diag_in(bb[1])], axis=2)
        wb_prev = jnp.concatenate([_block_diag_in(ab[0]), _block_diag_in(ab[1])], axis=2)
        wb2 = jnp.concatenate([wb_cur, wb_prev], axis=1).astype(BF16)
        wb1 = wb_cur.astype(BF16)
        a2r = _slice_rows(a2)
        a1r = _slice_rows(lb)
        cmat = jnp.concatenate([_block_diag_out(ssm_c_re[l]), -_block_diag_out(ssm_c_im[l])], axis=1).astype(BF16)
        dvec = ssm_d[l].reshape(N_SLICES, 1, LANES)

        proj_p, u_tm = _in_proj(x_p, norm_mix3, w_in, l, (seq, nb * WS), u_index_prompt, "in_proj_prompt")
        ya_p, cst_p = _conv_branch(proj_p, zero_conv, conv_w, w_out_conv, l, TM_CONV, 1, seq // TM_CONV,
                                   "conv_prompt")
        z_p, hs_p = _ssm_prompt(u_tm.reshape(n_p, WS), wb2, a2r, cmat, dvec, nb)
        d_p = _merge(z_p.reshape(seq, nb * WS), lambda i, j: (i % seq_blks, i // seq_blks),
                     ya_p, proj_p, b_gate3, w_glu, w_o, l, "merge_prompt")

        proj_s, u_s = _in_proj(x_s, norm_mix3, w_in, l, (n_s, WS), u_index_sample, "in_proj_sample")
        cinit = jnp.transpose(state_conv[l], (1, 0, 2)).reshape(1, 2 * n_seq, WC)
        ya_s, cst_s = _conv_branch(proj_s, cinit, conv_w, w_out_conv, l, n_s, n_seq, 1, "conv_sample")
        h0 = jnp.concatenate([state_ssm_re[l].reshape(n_seq, N_SLICES, SLICE_STATES),
                              state_ssm_im[l].reshape(n_seq, N_SLICES, SLICE_STATES)], axis=2)
        z_s, hs_s = _ssm_sample(u_s, h0.reshape(n_seq, N_SLICES * SLICE_COLS), wb1, a1r, cmat, dvec,
                                n_seq, n_steps)
        d_s = _merge(z_s, lambda i, j: (0, 0), ya_s, proj_s, b_gate3, w_glu, w_o, l, "merge_sample")

        conv_p.append(cst_p[:, SUBLANES - 2:, :])
        hs_p = hs_p[:, nb:, :].reshape(N_SLICES, nb, 2, GROUPS_PER_SLICE, SSM_P)
        hs_p = jnp.transpose(hs_p, (2, 1, 0, 3, 4)).reshape(2, nb, SSM_G, SSM_P)
        re_p.append(hs_p[0])
        im_p.append(hs_p[1])
        conv_s.append(jnp.transpose(cst_s.reshape(2, n_seq, WC), (1, 0, 2)))
        hs_s = hs_s.reshape(n_seq, N_SLICES, 2, GROUPS_PER_SLICE, SSM_P)
        hs_s = jnp.transpose(hs_s, (2, 0, 1, 3, 4)).reshape(2, n_seq, SSM_G, SSM_P)
        re_s.append(hs_s[0])
        im_s.append(hs_s[1])

        wr = jnp.zeros((LANES, dm), F32)
        wr = wr.at[0:N_GROUPS].set(w_router_group[l].T)
        wr = wr.at[SUBLANES:SUBLANES + N_EXPERTS].set(w_router_expert[l].T)
        whi = wr.astype(BF16)
        wlo = (wr - whi.astype(F32)).astype(BF16)
        br = jnp.zeros((LANES, 1), F32)
        br = br.at[0:N_GROUPS, 0].set(b_router_group[l])
        br = br.at[SUBLANES:SUBLANES + N_EXPERTS, 0].set(b_router_expert[l])
        x_mid, xn, eid, wts, rank, counts = _router(x_p, d_p, x_s, d_s, norm_ffn3, whi, wlo, br, l)
        pos_flat, tile_expert, n_used = _routing_plan(eid, rank, counts, n_tiles)
        sorted_token = _invert(pos_flat, t_all, n_tiles * TE)
        y = _experts(tile_expert, n_used, sorted_token, xn, w_exp_gate, w_exp_up, w_exp_down, l)
        final = l == depth - 1
        x_p, x_s = _combine(pos_flat, x_mid, jnp.transpose(wts), norm_final.reshape(1, dm), y, final, n_p)

    y_prompt = x_p.reshape(nb, seq, dm)
    y_sample = jnp.transpose(x_s.reshape(n_steps, n_seq, dm), (1, 0, 2))
    return (y_prompt, y_sample,
            jnp.stack(conv_p), jnp.stack(re_p), jnp.stack(im_p),
            jnp.stack(conv_s), jnp.stack(re_s), jnp.stack(im_s))
```

```python
import functools

import jax
import jax.numpy as jnp
from jax import lax
from jax.experimental import pallas as pl
from jax.experimental.pallas import tpu as pltpu

F32 = jnp.float32
BF16 = jnp.bfloat16
EPS = 1e-6

VMEM_LIMIT_BYTES = 56 * 1024 * 1024
SUBLANES = 8
LANES = 128

D_MODEL = 2048
WC = 1024
WS = 1024
SSM_H = 16
SSM_G = 64
SSM_P = 64
N_PROJ = 3 * WC + WS + 2 * D_MODEL
N_GROUPS = 4
EXPERTS_PER_GROUP = 8
N_EXPERTS = N_GROUPS * EXPERTS_PER_GROUP
TOP_K = 2
D_EXPERT = 512

GROUPS_PER_SLICE = LANES // SSM_H
N_SLICES = SSM_G // GROUPS_PER_SLICE
SLICE_STATES = GROUPS_PER_SLICE * SSM_P
SLICE_COLS = 2 * SLICE_STATES

TM_PROJ = 1024
TN_PROJ = 512
TM_CONV = 512
TN_CONV = 512
TM_SSM = 1024
TM_MERGE = 1024
TJ_MERGE = 512
TM_ROUTE = 512
TE = 256
TM_COMBINE = 256


def _cparams(*sem):
    return pltpu.CompilerParams(dimension_semantics=sem, vmem_limit_bytes=VMEM_LIMIT_BYTES)


def _rmsnorm(x, g):
    ms = jnp.mean(x * x, axis=-1, keepdims=True)
    return x * lax.rsqrt(ms + EPS) * g


SLAB_ROWS = D_MODEL // LANES


def _store_slabs(ref, x, row0=0):
    n = x.shape[0]
    for j in range(SLAB_ROWS):
        ref[pl.ds(row0 * SLAB_ROWS + j, n, stride=SLAB_ROWS), :] = x[:, j * LANES:(j + 1) * LANES]


def _load_slabs(ref, row0, n):
    parts = [ref[pl.ds(row0 * SLAB_ROWS + j, n, stride=SLAB_ROWS), :] for j in range(SLAB_ROWS)]
    return jnp.concatenate(parts, axis=1)


def _in_proj_kernel(x_ref, g_ref, w_ref, p_ref, u_ref, xn_ref, *, u_first, u_last):
    j = pl.program_id(1)

    @pl.when(j == 0)
    def _():
        xn_ref[...] = _rmsnorm(x_ref[...], g_ref[...]).astype(BF16)

    acc = jnp.dot(xn_ref[...], w_ref[...].astype(BF16), preferred_element_type=F32)
    p_ref[...] = acc

    @pl.when((j >= u_first) & (j <= u_last))
    def _():
        u_ref[...] = acc


def _in_proj(x, norm_g, w_in, layer, u_shape, u_index, name):
    n_row_blks = x.shape[0] // TM_PROJ
    u_first = (3 * WC) // TN_PROJ
    u_last = (3 * WC + WS) // TN_PROJ - 1
    kern = functools.partial(_in_proj_kernel, u_first=u_first, u_last=u_last)
    return pl.pallas_call(
        kern,
        grid=(n_row_blks, N_PROJ // TN_PROJ),
        in_specs=[
            pl.BlockSpec((TM_PROJ, D_MODEL), lambda i, j: (i, 0)),
            pl.BlockSpec((None, 1, D_MODEL), lambda i, j: (layer, 0, 0)),
            pl.BlockSpec((None, D_MODEL, TN_PROJ), lambda i, j: (layer, 0, j)),
        ],
        out_specs=[
            pl.BlockSpec((TM_PROJ, TN_PROJ), lambda i, j: (i, j)),
            pl.BlockSpec((TM_PROJ, TN_PROJ), functools.partial(u_index, u_first=u_first)),
        ],
        out_shape=[
            jax.ShapeDtypeStruct((n_row_blks * TM_PROJ, N_PROJ), F32),
            jax.ShapeDtypeStruct(u_shape, F32),
        ],
        scratch_shapes=[pltpu.VMEM((TM_PROJ, D_MODEL), BF16)],
        compiler_params=_cparams("arbitrary", "arbitrary"),
        name=name,
    )(x, norm_g, w_in)


def _conv_kernel(b_ref, c_ref, h_ref, init_ref, cw_ref, w_ref, ya_ref, st_ref, bc_ref, carry_ref,
                 *, shift, tiles_per_seq):
    i = pl.program_id(0)
    j = pl.program_id(1)
    tm = b_ref.shape[0]
    nc = carry_ref.shape[0]

    @pl.when(j == 0)
    def _():
        @pl.when(i % tiles_per_seq == 0)
        def _():
            carry_ref[...] = init_ref[...]

        v = c_ref[...] * h_ref[...]
        ext = jnp.concatenate([carry_ref[...], v], axis=0)
        v1 = ext[nc - shift:nc - shift + tm]
        v2 = ext[nc - 2 * shift:nc - 2 * shift + tm]
        cw = cw_ref[...]
        conv = cw[0:1] * v2 + cw[1:2] * v1 + cw[2:3] * v
        bc_ref[...] = (b_ref[...] * conv).astype(BF16)
        carry_ref[...] = v[tm - nc:]
        st_ref[...] = v[tm - nc:]

    ya_ref[...] = jnp.dot(bc_ref[...], w_ref[...].astype(BF16), preferred_element_type=F32)


def _conv_branch(proj, init, conv_w, w_out_conv, layer, tm, shift, tiles_per_seq, name):
    n_rows = proj.shape[0]
    nc = init.shape[1]
    n_seq = init.shape[0]
    kern = functools.partial(_conv_kernel, shift=shift, tiles_per_seq=tiles_per_seq)
    return pl.pallas_call(
        kern,
        grid=(n_rows // tm, D_MODEL // TN_CONV),
        in_specs=[
            pl.BlockSpec((tm, WC), lambda i, j: (i, 0)),
            pl.BlockSpec((tm, WC), lambda i, j: (i, 1)),
            pl.BlockSpec((tm, WC), lambda i, j: (i, 2)),
            pl.BlockSpec((None, nc, WC), lambda i, j: (i // tiles_per_seq, 0, 0)),
            pl.BlockSpec((None, 3, WC), lambda i, j: (layer, 0, 0)),
            pl.BlockSpec((None, WC, TN_CONV), lambda i, j: (layer, 0, j)),
        ],
        out_specs=[
            pl.BlockSpec((tm, TN_CONV), lambda i, j: (i, j)),
            pl.BlockSpec((None, nc, WC), lambda i, j: (i // tiles_per_seq, 0, 0)),
        ],
        out_shape=[
            jax.ShapeDtypeStruct((n_rows, D_MODEL), F32),
            jax.ShapeDtypeStruct((n_seq, nc, WC), F32),
        ],
        scratch_shapes=[pltpu.VMEM((tm, WC), BF16), pltpu.VMEM((nc, WC), F32)],
        compiler_params=_cparams("arbitrary", "arbitrary"),
        name=name,
    )(proj, proj, proj, init, conv_w, w_out_conv)


def _disc_kernel(lr_ref, li_ref, ls_ref, br_ref, bi_ref, lb_ref, a2_ref, bb_ref, ab_ref):
    lr = lr_ref[...]
    li = li_ref[...]
    dt = jnp.exp(ls_ref[...])
    mag = jnp.exp(lr * dt)
    lbr = mag * jnp.cos(li * dt)
    lbi = mag * jnp.sin(li * dt)
    nr, ni = lbr - 1.0, lbi
    den = lr * lr + li * li
    fr = (nr * lr + ni * li) / den
    fi = (ni * lr - nr * li) / den
    lb_ref[0] = lbr
    lb_ref[1] = lbi
    a2_ref[0] = lbr * lbr - lbi * lbi
    a2_ref[1] = 2.0 * lbr * lbi
    frt = jnp.concatenate([fr] * SSM_H, axis=0)
    fit = jnp.concatenate([fi] * SSM_H, axis=0)
    lbrt = jnp.concatenate([lbr] * SSM_H, axis=0)
    lbit = jnp.concatenate([lbi] * SSM_H, axis=0)
    br = br_ref[...]
    bi = bi_ref[...]
    bbr = frt * br - fit * bi
    bbi = frt * bi + fit * br
    bb_ref[0] = bbr
    bb_ref[1] = bbi
    ab_ref[0] = lbrt * bbr - lbit * bbi
    ab_ref[1] = lbrt * bbi + lbit * bbr


def _discretize(lam_re, lam_im, log_step, b_re, b_im):
    g, p = lam_re.shape
    hg = SSM_H * g
    brt = jnp.transpose(b_re, (2, 0, 1)).reshape(hg, p)
    bit = jnp.transpose(b_im, (2, 0, 1)).reshape(hg, p)
    return pl.pallas_call(
        _disc_kernel,
        out_shape=[
            jax.ShapeDtypeStruct((2, g, p), F32),
            jax.ShapeDtypeStruct((2, g, p), F32),
            jax.ShapeDtypeStruct((2, hg, p), F32),
            jax.ShapeDtypeStruct((2, hg, p), F32),
        ],
        name="s5_discretize",
    )(lam_re, lam_im, log_step.reshape(g, 1), brt, bit)


def _block_diag_in(m):
    m = m.reshape(SSM_H, N_SLICES, GROUPS_PER_SLICE, SSM_P)
    eye = jnp.eye(GROUPS_PER_SLICE, dtype=m.dtype)
    out = jnp.einsum('hkgp,gq->kghqp', m, eye)
    return out.reshape(N_SLICES, LANES, SLICE_STATES)


def _block_diag_out(c):
    c = c.reshape(N_SLICES, GROUPS_PER_SLICE, SSM_H, SSM_P)
    eye = jnp.eye(GROUPS_PER_SLICE, dtype=c.dtype)
    out = jnp.einsum('kghp,gq->kgpqh', c, eye)
    return out.reshape(N_SLICES, SLICE_STATES, LANES)


def _slice_rows(a):
    a = a.reshape(2, N_SLICES, SLICE_STATES)
    a = jnp.transpose(a, (1, 0, 2)).reshape(N_SLICES, 1, SLICE_COLS)
    return jnp.broadcast_to(a, (N_SLICES, SUBLANES, SLICE_COLS))


def _gelu_out(x_state, u, c_ref, d_ref):
    y = jnp.dot(x_state.astype(BF16), c_ref[...], preferred_element_type=F32)
    s = y + d_ref[...] * u
    return jax.nn.gelu(s).astype(BF16)


def _ssm_prompt_kernel(u_ref, wb_ref, a_ref, c_ref, d_ref, z_ref, hs_ref, v_ref, st_ref, up_ref, *, nb):
    tb = pl.program_id(1)
    tm = u_ref.shape[0]
    half = SLICE_STATES

    @pl.when(tb == 0)
    def _():
        st_ref[...] = jnp.zeros_like(st_ref)
        up_ref[...] = jnp.zeros_like(up_ref)

    u = u_ref[...]
    ext = jnp.concatenate([up_ref[...], u], axis=0)
    u_prev = ext[SUBLANES - nb:SUBLANES - nb + tm]
    lhs = jnp.concatenate([u, u_prev], axis=1).astype(BF16)
    v_ref[...] = jnp.dot(lhs, wb_ref[...], preferred_element_type=F32)
    up_ref[...] = u[tm - SUBLANES:]

    a_re = a_ref[:, :half]
    a_im = a_ref[:, half:]

    def body(s, carry):
        sre, sim = carry
        r0 = pl.multiple_of(s * SUBLANES, SUBLANES)
        vre = v_ref[pl.ds(r0, SUBLANES), :half]
        vim = v_ref[pl.ds(r0, SUBLANES), half:]
        nre = a_re * sre - a_im * sim + vre
        nim = a_re * sim + a_im * sre + vim
        v_ref[pl.ds(r0, SUBLANES), :half] = nre
        v_ref[pl.ds(r0, SUBLANES), half:] = nim
        return nre, nim

    sre, sim = lax.fori_loop(0, tm // SUBLANES, body, (st_ref[:, :half], st_ref[:, half:]))
    st_ref[:, :half] = sre
    st_ref[:, half:] = sim
    hs_ref[:, :half] = sre
    hs_ref[:, half:] = sim
    z_ref[...] = _gelu_out(v_ref[...], u, c_ref, d_ref)


def _ssm_prompt(u_tm, wb, a2, cmat, dvec, nb):
    n_rows = u_tm.shape[0]
    kern = functools.partial(_ssm_prompt_kernel, nb=nb)
    return pl.pallas_call(
        kern,
        grid=(N_SLICES, n_rows // TM_SSM),
        in_specs=[
            pl.BlockSpec((TM_SSM, LANES), lambda k, t: (t, k)),
            pl.BlockSpec((None, 2 * LANES, SLICE_COLS), lambda k, t: (k, 0, 0)),
            pl.BlockSpec((None, SUBLANES, SLICE_COLS), lambda k, t: (k, 0, 0)),
            pl.BlockSpec((None, SLICE_COLS, LANES), lambda k, t: (k, 0, 0)),
            pl.BlockSpec((None, 1, LANES), lambda k, t: (k, 0, 0)),
        ],
        out_specs=[
            pl.BlockSpec((TM_SSM, LANES), lambda k, t: (t, k)),
            pl.BlockSpec((None, SUBLANES, SLICE_COLS), lambda k, t: (k, 0, 0)),
        ],
        out_shape=[
            jax.ShapeDtypeStruct((n_rows, WS), BF16),
            jax.ShapeDtypeStruct((N_SLICES, SUBLANES, SLICE_COLS), F32),
        ],
        scratch_shapes=[
            pltpu.VMEM((TM_SSM, SLICE_COLS), F32),
            pltpu.VMEM((SUBLANES, SLICE_COLS), F32),
            pltpu.VMEM((SUBLANES, LANES), F32),
        ],
        compiler_params=_cparams("arbitrary", "arbitrary"),
        name="s5_prompt",
    )(u_tm, wb, a2, cmat, dvec)


def _ssm_sample_kernel(u_ref, h0_ref, wb_ref, a_ref, c_ref, d_ref, z_ref, hs_ref, v_ref, *, n_seq, n_steps):
    half = SLICE_STATES
    u = u_ref[...]
    v_ref[...] = jnp.dot(u.astype(BF16), wb_ref[...], preferred_element_type=F32)
    a_re = a_ref[:, :half]
    a_im = a_ref[:, half:]

    def body(q, _):
        n0 = pl.multiple_of(q * SUBLANES, SUBLANES)
        sre = h0_ref[pl.ds(n0, SUBLANES), :half]
        sim = h0_ref[pl.ds(n0, SUBLANES), half:]
        for t in range(n_steps):
            r0 = pl.multiple_of(t * n_seq + n0, SUBLANES)
            vre = v_ref[pl.ds(r0, SUBLANES), :half]
            vim = v_ref[pl.ds(r0, SUBLANES), half:]
            nre = a_re * sre - a_im * sim + vre
            nim = a_re * sim + a_im * sre + vim
            v_ref[pl.ds(r0, SUBLANES), :half] = nre
            v_ref[pl.ds(r0, SUBLANES), half:] = nim
            sre, sim = nre, nim
        hs_ref[pl.ds(n0, SUBLANES), :half] = sre
        hs_ref[pl.ds(n0, SUBLANES), half:] = sim
        return 0

    lax.fori_loop(0, n_seq // SUBLANES, body, 0)
    z_ref[...] = _gelu_out(v_ref[...], u, c_ref, d_ref)


def _ssm_sample(u_s, h0, wb1, a1, cmat, dvec, n_seq, n_steps):
    n_rows = u_s.shape[0]
    kern = functools.partial(_ssm_sample_kernel, n_seq=n_seq, n_steps=n_steps)
    return pl.pallas_call(
        kern,
        grid=(N_SLICES,),
        in_specs=[
            pl.BlockSpec((n_rows, LANES), lambda k: (0, k)),
            pl.BlockSpec((n_seq, SLICE_COLS), lambda k: (0, k)),
            pl.BlockSpec((None, LANES, SLICE_COLS), lambda k: (k, 0, 0)),
            pl.BlockSpec((None, SUBLANES, SLICE_COLS), lambda k: (k, 0, 0)),
            pl.BlockSpec((None, SLICE_COLS, LANES), lambda k: (k, 0, 0)),
            pl.BlockSpec((None, 1, LANES), lambda k: (k, 0, 0)),
        ],
        out_specs=[
            pl.BlockSpec((n_rows, LANES), lambda k: (0, k)),
            pl.BlockSpec((n_seq, SLICE_COLS), lambda k: (0, k)),
        ],
        out_shape=[
            jax.ShapeDtypeStruct((n_rows, WS), BF16),
            jax.ShapeDtypeStruct((n_seq, N_SLICES * SLICE_COLS), F32),
        ],
        scratch_shapes=[pltpu.VMEM((n_rows, SLICE_COLS), F32)],
        compiler_params=_cparams("arbitrary"),
        name="s5_sample",
    )(u_s, h0, wb1, a1, cmat, dvec)


def _merge_kernel(z_ref, ya_ref, pa_ref, pb_ref, bga_ref, bgb_ref, wga_ref, wgb_ref, wo_ref, o_ref):
    j = pl.program_id(1)
    z = z_ref[...]
    glu_a = jnp.dot(z, wga_ref[...].astype(BF16), preferred_element_type=F32)
    glu_b = jnp.dot(z, wgb_ref[...].astype(BF16), preferred_element_type=F32)
    yb = glu_a * jax.nn.sigmoid(glu_b)
    g1 = jax.nn.sigmoid(pa_ref[...] + bga_ref[...])
    g2 = jax.nn.sigmoid(pb_ref[...] + bgb_ref[...])
    merged = (g1 * ya_ref[...] + g2 * yb).astype(BF16)
    part = jnp.dot(merged, wo_ref[...].astype(BF16), preferred_element_type=F32)

    @pl.when(j == 0)
    def _():
        o_ref[...] = part

    @pl.when(j > 0)
    def _():
        o_ref[...] += part


def _merge(z, z_index, ya, proj, b_gate, w_glu, w_o, layer, name):
    n_rows = ya.shape[0]
    nj = D_MODEL // TJ_MERGE
    ga0 = (3 * WC + WS) // TJ_MERGE
    gb0 = ga0 + nj
    in_specs = [
        pl.BlockSpec((TM_MERGE, WS), z_index),
        pl.BlockSpec((TM_MERGE, TJ_MERGE), lambda i, j: (i, j)),
        pl.BlockSpec((TM_MERGE, TJ_MERGE), lambda i, j: (i, ga0 + j)),
        pl.BlockSpec((TM_MERGE, TJ_MERGE), lambda i, j: (i, gb0 + j)),
        pl.BlockSpec((None, 1, TJ_MERGE), lambda i, j: (layer, 0, j)),
        pl.BlockSpec((None, 1, TJ_MERGE), lambda i, j: (layer, 0, nj + j)),
        pl.BlockSpec((None, WS, TJ_MERGE), lambda i, j: (layer, 0, j)),
        pl.BlockSpec((None, WS, TJ_MERGE), lambda i, j: (layer, 0, nj + j)),
        pl.BlockSpec((None, TJ_MERGE, D_MODEL), lambda i, j: (layer, j, 0)),
    ]
    return pl.pallas_call(
        _merge_kernel,
        grid=(n_rows // TM_MERGE, nj),
        in_specs=in_specs,
        out_specs=pl.BlockSpec((TM_MERGE, D_MODEL), lambda i, j: (i, 0)),
        out_shape=jax.ShapeDtypeStruct((n_rows, D_MODEL), F32),
        compiler_params=_cparams("arbitrary", "arbitrary"),
        name=name,
    )(z, ya, proj, proj, b_gate, b_gate, w_glu, w_glu, w_o)


def _router_kernel(xp_ref, dp_ref, xs_ref, ds_ref, g_ref, whi_ref, wlo_ref, br_ref,
                   xmid_ref, xn_ref, eid_ref, wts_ref, rank_ref, cnt_ref, run_ref, *, np_blocks):
    i = pl.program_id(0)
    tm = xp_ref.shape[0]

    @pl.when(i == 0)
    def _():
        run_ref[...] = jnp.zeros_like(run_ref)

    @pl.when(i < np_blocks)
    def _():
        xmid_ref[...] = xp_ref[...] + dp_ref[...]

    @pl.when(i >= np_blocks)
    def _():
        xmid_ref[...] = xs_ref[...] + ds_ref[...]

    xm = xmid_ref[...]
    xn = _rmsnorm(xm, g_ref[...])
    _store_slabs(xn_ref, xn)
    hi = xn.astype(BF16)
    lo = (xn - hi.astype(F32)).astype(BF16)
    nt = (((1,), (1,)), ((), ()))
    whi = whi_ref[...]
    wlo = wlo_ref[...]
    lt = (lax.dot_general(whi, hi, nt, preferred_element_type=F32)
          + lax.dot_general(wlo, hi, nt, preferred_element_type=F32)
          + lax.dot_general(whi, lo, nt, preferred_element_type=F32))
    lt = lt + br_ref[...]

    neg = jnp.float32(-jnp.inf)
    row8 = lax.broadcasted_iota(jnp.int32, (SUBLANES, tm), 0)
    lg = jnp.where(row8 < N_GROUPS, lt[0:SUBLANES], neg)
    gmax = jnp.max(lg, axis=0, keepdims=True)
    gi = jnp.min(jnp.where(lg == gmax, row8, SUBLANES), axis=0, keepdims=True)
    gp = 1.0 / jnp.sum(jnp.exp(lg - gmax), axis=0, keepdims=True)

    sel = lt[SUBLANES:2 * SUBLANES]
    for g in range(1, N_GROUPS):
        sel = jnp.where(gi == g, lt[(g + 1) * SUBLANES:(g + 2) * SUBLANES], sel)
    m1 = jnp.max(sel, axis=0, keepdims=True)
    i1 = jnp.min(jnp.where(sel == m1, row8, SUBLANES), axis=0, keepdims=True)
    sel2 = jnp.where(row8 == i1, neg, sel)
    m2 = jnp.max(sel2, axis=0, keepdims=True)
    i2 = jnp.min(jnp.where(sel2 == m2, row8, SUBLANES), axis=0, keepdims=True)
    e = jnp.exp(m2 - m1)
    w1 = gp / (1.0 + e)
    w2 = gp * e / (1.0 + e)
    e1 = gi * EXPERTS_PER_GROUP + i1
    e2 = gi * EXPERTS_PER_GROUP + i2
    eid_ref[0:1, :] = e1
    eid_ref[1:2, :] = e2
    wts_ref[0:1, :] = w1
    wts_ref[1:2, :] = w2

    erow = lax.broadcasted_iota(jnp.int32, (N_EXPERTS, tm), 0)
    oh1 = jnp.where(erow == e1, 1.0, 0.0)
    oh2 = jnp.where(erow == e2, 1.0, 0.0)
    oh = jnp.concatenate([oh1, oh2], axis=0).astype(BF16)
    tri = jnp.where(lax.broadcasted_iota(jnp.int32, (tm, tm), 0) < lax.broadcasted_iota(jnp.int32, (tm, tm), 1),
                    1.0, 0.0).astype(BF16)
    pref = jnp.dot(oh, tri, preferred_element_type=F32)
    p1 = pref[:N_EXPERTS]
    p2 = pref[N_EXPERTS:]
    tot1 = jnp.sum(oh1, axis=1, keepdims=True)
    tot2 = jnp.sum(oh2, axis=1, keepdims=True)
    run = run_ref[...]
    r1 = jnp.sum(oh1 * (run + p1), axis=0, keepdims=True)
    r2 = jnp.sum(oh2 * (run + tot1 + p2), axis=0, keepdims=True)
    rank_ref[0:1, :] = r1.astype(jnp.int32)
    rank_ref[1:2, :] = r2.astype(jnp.int32)
    run = run + tot1 + tot2
    run_ref[...] = run
    cnt_ref[...] = run


def _router(x_p, d_p, x_s, d_s, norm_g, whi, wlo, br, layer):
    np_blocks = x_p.shape[0] // TM_ROUTE
    t = x_p.shape[0] + x_s.shape[0]
    row = lambda i: (i, 0)
    row_p = lambda i: (jnp.minimum(i, np_blocks - 1), 0)
    row_s = lambda i: (jnp.maximum(i - np_blocks, 0), 0)
    col = lambda i: (0, i)
    fixed = lambda i: (0, 0)
    return pl.pallas_call(
        functools.partial(_router_kernel, np_blocks=np_blocks),
        grid=(t // TM_ROUTE,),
        in_specs=[
            pl.BlockSpec((TM_ROUTE, D_MODEL), row_p),
            pl.BlockSpec((TM_ROUTE, D_MODEL), row_p),
            pl.BlockSpec((TM_ROUTE, D_MODEL), row_s),
            pl.BlockSpec((TM_ROUTE, D_MODEL), row_s),
            pl.BlockSpec((None, 1, D_MODEL), lambda i: (layer, 0, 0)),
            pl.BlockSpec((LANES, D_MODEL), fixed),
            pl.BlockSpec((LANES, D_MODEL), fixed),
            pl.BlockSpec((LANES, 1), fixed),
        ],
        out_specs=[
            pl.BlockSpec((TM_ROUTE, D_MODEL), row),
            pl.BlockSpec((TM_ROUTE * SLAB_ROWS, LANES), row),
            pl.BlockSpec((TOP_K, TM_ROUTE), col),
            pl.BlockSpec((TOP_K, TM_ROUTE), col),
            pl.BlockSpec((TOP_K, TM_ROUTE), col),
            pl.BlockSpec((N_EXPERTS, 1), fixed),
        ],
        out_shape=[
            jax.ShapeDtypeStruct((t, D_MODEL), F32),
            jax.ShapeDtypeStruct((t * SLAB_ROWS, LANES), F32),
            jax.ShapeDtypeStruct((TOP_K, t), jnp.int32),
            jax.ShapeDtypeStruct((TOP_K, t), F32),
            jax.ShapeDtypeStruct((TOP_K, t), jnp.int32),
            jax.ShapeDtypeStruct((N_EXPERTS, 1), F32),
        ],
        scratch_shapes=[pltpu.VMEM((N_EXPERTS, 1), F32)],
        compiler_params=_cparams("arbitrary"),
        name="moe_router",
    )(x_p, d_p, x_s, d_s, norm_g, whi, wlo, br)


def _invert_kernel(pos_ref, sorted_ref, *, n_tokens):
    def clear(p, _):
        sorted_ref[p] = 0
        return 0

    lax.fori_loop(0, sorted_ref.shape[0], clear, 0, unroll=8)

    def place(t, _):
        for k in range(TOP_K):
            sorted_ref[pos_ref[k * n_tokens + t]] = t
        return 0

    lax.fori_loop(0, n_tokens, place, 0, unroll=8)


def _invert(pos_flat, n_tokens, n_sorted):
    return pl.pallas_call(
        functools.partial(_invert_kernel, n_tokens=n_tokens),
        in_specs=[pl.BlockSpec(memory_space=pltpu.SMEM)],
        out_specs=pl.BlockSpec(memory_space=pltpu.SMEM),
        out_shape=jax.ShapeDtypeStruct((n_sorted,), jnp.int32),
        name="moe_invert",
    )(pos_flat)


def _experts_kernel(te_ref, nx_ref, nu_ref, st_ref, xn_hbm, wg_hbm, wu_hbm, wd_hbm, y_ref,
                    wgf, wuf, wdf, wgs, wus, wds, xbuf, xsems, wsems, *, layer):
    i = pl.program_id(0)
    slot = i % 2
    slab_tile = TE * SLAB_ROWS

    def gather(tile, s):
        base = tile * TE

        def body(r, _):
            src = pl.multiple_of(st_ref[base + r] * SLAB_ROWS, SLAB_ROWS)
            dst = pl.multiple_of((s * TE + r) * SLAB_ROWS, SLAB_ROWS)
            pltpu.make_async_copy(xn_hbm.at[pl.ds(src, SLAB_ROWS), :],
                                  xbuf.at[pl.ds(dst, SLAB_ROWS), :], xsems.at[s]).start()
            return 0

        lax.fori_loop(0, TE, body, 0, unroll=8)

    def weight_copies(e):
        return (pltpu.make_async_copy(wg_hbm.at[layer, e], wgf, wsems.at[0]),
                pltpu.make_async_copy(wu_hbm.at[layer, e], wuf, wsems.at[1]),
                pltpu.make_async_copy(wd_hbm.at[layer, e], wdf, wsems.at[2]))

    @pl.when(i == 0)
    def _():
        gather(0, 0)
        for cp in weight_copies(te_ref[0]):
            cp.start()

    @pl.when(i + 1 < nu_ref[0])
    def _():
        gather(i + 1, 1 - slot)

    @pl.when(i < nu_ref[0])
    def _():
        prev = te_ref[jnp.maximum(i - 1, 0)]

        @pl.when((i == 0) | (te_ref[i] != prev))
        def _():
            for cp in weight_copies(te_ref[i]):
                cp.wait()
            wgs[...] = wgf[...].astype(BF16)
            wus[...] = wuf[...].astype(BF16)
            wds[...] = wdf[...].astype(BF16)

            @pl.when(nx_ref[i] >= 0)
            def _():
                for cp in weight_copies(nx_ref[i]):
                    cp.start()

        x0 = pl.multiple_of(slot * slab_tile, slab_tile)
        pltpu.make_async_copy(xn_hbm.at[pl.ds(0, slab_tile), :], xbuf.at[pl.ds(x0, slab_tile), :],
                              xsems.at[slot]).wait()
        x = _load_slabs(xbuf, slot * TE, TE).astype(BF16)
        hg = jnp.dot(x, wgs[...], preferred_element_type=F32)
        hu = jnp.dot(x, wus[...], preferred_element_type=F32)
        h = (hg * jax.nn.sigmoid(hg) * hu).astype(BF16)
        _store_slabs(y_ref, jnp.dot(h, wds[...], preferred_element_type=F32))

    @pl.when(i >= nu_ref[0])
    def _():
        y_ref[...] = jnp.zeros_like(y_ref)


def _experts(tile_expert, next_expert, n_used, sorted_token, xn_slabs, w_gate, w_up, w_down, layer):
    n_tiles = sorted_token.shape[0] // TE
    hbm = pl.BlockSpec(memory_space=pl.ANY)
    return pl.pallas_call(
        functools.partial(_experts_kernel, layer=layer),
        grid_spec=pltpu.PrefetchScalarGridSpec(
            num_scalar_prefetch=4,
            grid=(n_tiles,),
            in_specs=[hbm, hbm, hbm, hbm],
            out_specs=pl.BlockSpec((TE * SLAB_ROWS, LANES), lambda i, te, nx, nu, st: (i, 0)),
            scratch_shapes=[
                pltpu.VMEM((D_MODEL, D_EXPERT), F32),
                pltpu.VMEM((D_MODEL, D_EXPERT), F32),
                pltpu.VMEM((D_EXPERT, D_MODEL), F32),
                pltpu.VMEM((D_MODEL, D_EXPERT), BF16),
                pltpu.VMEM((D_MODEL, D_EXPERT), BF16),
                pltpu.VMEM((D_EXPERT, D_MODEL), BF16),
                pltpu.VMEM((2 * TE * SLAB_ROWS, LANES), F32),
                pltpu.SemaphoreType.DMA((2,)),
                pltpu.SemaphoreType.DMA((3,)),
            ],
        ),
        out_shape=jax.ShapeDtypeStruct((n_tiles * TE * SLAB_ROWS, LANES), F32),
        compiler_params=_cparams("arbitrary"),
        name="moe_experts",
    )(tile_expert, next_expert, n_used, sorted_token, xn_slabs, w_gate, w_up, w_down)


def _combine_kernel(pos_ref, x_ref, w_ref, gf_ref, y_hbm, op_ref, os_ref, ybuf, sems,
                    *, n_tokens, final, np_blocks):
    i = pl.program_id(0)
    n = pl.num_programs(0)
    slot = i % 2
    tm = x_ref.shape[0]

    def issue(step, s):
        base = step * tm

        def body(r, _):
            for k in range(TOP_K):
                src = pl.multiple_of(pos_ref[k * n_tokens + base + r] * SLAB_ROWS, SLAB_ROWS)
                dst = pl.multiple_of(((s * TOP_K + k) * tm + r) * SLAB_ROWS, SLAB_ROWS)
                pltpu.make_async_copy(y_hbm.at[pl.ds(src, SLAB_ROWS), :],
                                      ybuf.at[pl.ds(dst, SLAB_ROWS), :], sems.at[s]).start()
            return 0

        lax.fori_loop(0, tm, body, 0, unroll=8)

    @pl.when(i == 0)
    def _():
        issue(0, 0)

    @pl.when(i + 1 < n)
    def _():
        issue(i + 1, 1 - slot)

    slab_slot = TOP_K * tm * SLAB_ROWS
    y0 = pl.multiple_of(slot * slab_slot, slab_slot)
    pltpu.make_async_copy(y_hbm.at[pl.ds(0, slab_slot), :], ybuf.at[pl.ds(y0, slab_slot), :], sems.at[slot]).wait()

    w = w_ref[...]
    out = (x_ref[...] + w[:, 0:1] * _load_slabs(ybuf, (slot * TOP_K) * tm, tm)
           + w[:, 1:2] * _load_slabs(ybuf, (slot * TOP_K + 1) * tm, tm))
    if final:
        out = _rmsnorm(out, gf_ref[...])

    @pl.when(i < np_blocks)
    def _():
        op_ref[...] = out

    @pl.when(i >= np_blocks)
    def _():
        os_ref[...] = out


def _combine(pos_flat, x_mid, w_t, g_final, y, final, n_prompt):
    t = x_mid.shape[0]
    np_blocks = n_prompt // TM_COMBINE
    row = lambda i, pos: (i, 0)
    out_specs = [
        pl.BlockSpec((TM_COMBINE, D_MODEL), lambda i, pos: (jnp.minimum(i, np_blocks - 1), 0)),
        pl.BlockSpec((TM_COMBINE, D_MODEL), lambda i, pos: (jnp.maximum(i - np_blocks, 0), 0)),
    ]
    out_shape = [
        jax.ShapeDtypeStruct((n_prompt, D_MODEL), F32),
        jax.ShapeDtypeStruct((t - n_prompt, D_MODEL), F32),
    ]
    return pl.pallas_call(
        functools.partial(_combine_kernel, n_tokens=t, final=final, np_blocks=np_blocks),
        grid_spec=pltpu.PrefetchScalarGridSpec(
            num_scalar_prefetch=1,
            grid=(t // TM_COMBINE,),
            in_specs=[
                pl.BlockSpec((TM_COMBINE, D_MODEL), row),
                pl.BlockSpec((TM_COMBINE, TOP_K), row),
                pl.BlockSpec((1, D_MODEL), lambda i, pos: (0, 0)),
                pl.BlockSpec(memory_space=pl.ANY),
            ],
            out_specs=out_specs,
            scratch_shapes=[
                pltpu.VMEM((2 * TOP_K * TM_COMBINE * SLAB_ROWS, LANES), F32),
                pltpu.SemaphoreType.DMA((2,)),
            ],
        ),
        out_shape=out_shape,
        compiler_params=_cparams("arbitrary"),
        name="moe_combine_final" if final else "moe_combine",
    )(pos_flat, x_mid, w_t, g_final, y)


def _routing_plan(eid, rank, counts, n_tiles):
    cnt = counts[:, 0].astype(jnp.int32)
    padded = ((cnt + TE - 1) // TE) * TE
    ends = jnp.cumsum(padded)
    off = ends - padded
    n_used = (ends[-1] // TE).astype(jnp.int32)
    experts = jnp.arange(N_EXPERTS, dtype=jnp.int32)
    pos = rank + jnp.sum(jnp.where(eid[..., None] == experts, off, 0), axis=-1)
    tile_start = jnp.arange(n_tiles, dtype=jnp.int32) * TE
    te = jnp.sum((tile_start[:, None] >= ends[None, :]).astype(jnp.int32), axis=1)
    last_expert = jnp.max(jnp.where(cnt > 0, experts, 0))
    te = jnp.where(tile_start < ends[-1], te, last_expert)
    later = (experts[None, :] > te[:, None]) & (cnt[None, :] > 0)
    nxt = jnp.min(jnp.where(later, experts[None, :], N_EXPERTS), axis=1)
    nxt = jnp.where(nxt < N_EXPERTS, nxt, -1).astype(jnp.int32)
    return pos.reshape(-1), te, nxt, n_used.reshape(1)


def kernel(x_prompt, x_sample, state_conv, state_ssm_re, state_ssm_im, norm_mix, w_in, b_gate, conv_w, w_out_conv, ssm_lambda_re, ssm_lambda_im, ssm_log_step, ssm_b_re, ssm_b_im, ssm_c_re, ssm_c_im, ssm_d, w_glu, w_o, norm_ffn, w_router_group, b_router_group, w_router_expert, b_router_expert, w_exp_gate, w_exp_up, w_exp_down, norm_final):
    nb, seq, dm = x_prompt.shape
    n_seq, n_steps, _ = x_sample.shape
    depth = w_in.shape[0]
    assert dm == D_MODEL and w_in.shape[2] == N_PROJ and w_exp_gate.shape[1:] == (N_EXPERTS, D_MODEL, D_EXPERT)
    assert ssm_lambda_re.shape[1:] == (SSM_G, SSM_P) and ssm_b_re.shape[3] == SSM_H
    assert 2 * nb == SUBLANES and n_seq % SUBLANES == 0
    n_p = nb * seq
    n_s = n_seq * n_steps
    t_all = n_p + n_s
    assert seq % TM_PROJ == 0 and n_s == TM_PROJ and n_s == TM_MERGE and seq % TM_CONV == 0
    assert n_p % TM_ROUTE == 0 and n_s % TM_ROUTE == 0 and n_p % TM_COMBINE == 0 and n_s % TM_COMBINE == 0
    n_tiles = (TOP_K * t_all + N_EXPERTS * (TE - 1)) // TE

    x_p = x_prompt.reshape(n_p, dm)
    x_s = jnp.transpose(x_sample, (1, 0, 2)).reshape(n_s, dm)
    norm_mix3 = norm_mix.reshape(depth, 1, dm)
    norm_ffn3 = norm_ffn.reshape(depth, 1, dm)
    b_gate3 = b_gate.reshape(depth, 1, 2 * dm)
    zero_conv = jnp.zeros((nb, SUBLANES, WC), F32)
    seq_blks = seq // TM_PROJ

    def u_index_prompt(i, j, u_first):
        return (i % seq_blks, (i // seq_blks) * (WS // TN_PROJ) + jnp.clip(j - u_first, 0, WS // TN_PROJ - 1))

    def u_index_sample(i, j, u_first):
        return (0, jnp.clip(j - u_first, 0, WS // TN_PROJ - 1))

    conv_p, re_p, im_p, conv_s, re_s, im_s = [], [], [], [], [], []
    for l in range(depth):
        lb, a2, bb, ab = _discretize(ssm_lambda_re[l], ssm_lambda_im[l], ssm_log_step[l], ssm_b_re[l], ssm_b_im[l])
        bb = bb.reshape(2, SSM_H, SSM_G, SSM_P)
        ab = ab.reshape(2, SSM_H, SSM_G, SSM_P)
        wb_cur = jnp.concatenate([_block_diag_in(bb[0]), _block_diag_in(bb[1])], axis=2)
        wb_prev = jnp.concatenate([_block_diag_in(ab[0]), _block_diag_in(ab[1])], axis=2)
        wb2 = jnp.concatenate([wb_cur, wb_prev], axis=1).astype(BF16)
        wb1 = wb_cur.astype(BF16)
        a2r = _slice_rows(a2)
        a1r = _slice_rows(lb)
        cmat = jnp.concatenate([_block_diag_out(ssm_c_re[l]), -_block_diag_out(ssm_c_im[l])], axis=1).astype(BF16)
        dvec = ssm_d[l].reshape(N_SLICES, 1, LANES)

        proj_p, u_tm = _in_proj(x_p, norm_mix3, w_in, l, (seq, nb * WS), u_index_prompt, "in_proj_prompt")
        ya_p, cst_p = _conv_branch(proj_p, zero_conv, conv_w, w_out_conv, l, TM_CONV, 1, seq // TM_CONV,
                                   "conv_prompt")
        z_p, hs_p = _ssm_prompt(u_tm.reshape(n_p, WS), wb2, a2r, cmat, dvec, nb)
        d_p = _merge(z_p.reshape(seq, nb * WS), lambda i, j: (i % seq_blks, i // seq_blks),
                     ya_p, proj_p, b_gate3, w_glu, w_o, l, "merge_prompt")

        proj_s, u_s = _in_proj(x_s, norm_mix3, w_in, l, (n_s, WS), u_index_sample, "in_proj_sample")
        cinit = jnp.transpose(state_conv[l], (1, 0, 2)).reshape(1, 2 * n_seq, WC)
        ya_s, cst_s = _conv_branch(proj_s, cinit, conv_w, w_out_conv, l, n_s, n_seq, 1, "conv_sample")
        h0 = jnp.concatenate([state_ssm_re[l].reshape(n_seq, N_SLICES, SLICE_STATES),
                              state_ssm_im[l].reshape(n_seq, N_SLICES, SLICE_STATES)], axis=2)
        z_s, hs_s = _ssm_sample(u_s, h0.reshape(n_seq, N_SLICES * SLICE_COLS), wb1, a1r, cmat, dvec,
                                n_seq, n_steps)
        d_s = _merge(z_s, lambda i, j: (0, 0), ya_s, proj_s, b_gate3, w_glu, w_o, l, "merge_sample")

        conv_p.append(cst_p[:, SUBLANES - 2:, :])
        hs_p = hs_p[:, nb:, :].reshape(N_SLICES, nb, 2, GROUPS_PER_SLICE, SSM_P)
        hs_p = jnp.transpose(hs_p, (2, 1, 0, 3, 4)).reshape(2, nb, SSM_G, SSM_P)
        re_p.append(hs_p[0])
        im_p.append(hs_p[1])
        conv_s.append(jnp.transpose(cst_s.reshape(2, n_seq, WC), (1, 0, 2)))
        hs_s = hs_s.reshape(n_seq, N_SLICES, 2, GROUPS_PER_SLICE, SSM_P)
        hs_s = jnp.transpose(hs_s, (2, 0, 1, 3, 4)).reshape(2, n_seq, SSM_G, SSM_P)
        re_s.append(hs_s[0])
        im_s.append(hs_s[1])

        wr = jnp.zeros((LANES, dm), F32)
        wr = wr.at[0:N_GROUPS].set(w_router_group[l].T)
        wr = wr.at[SUBLANES:SUBLANES + N_EXPERTS].set(w_router_expert[l].T)
        whi = wr.astype(BF16)
        wlo = (wr - whi.astype(F32)).astype(BF16)
        br = jnp.zeros((LANES, 1), F32)
        br = br.at[0:N_GROUPS, 0].set(b_router_group[l])
        br = br.at[SUBLANES:SUBLANES + N_EXPERTS, 0].set(b_router_expert[l])
        x_mid, xn, eid, wts, rank, counts = _router(x_p, d_p, x_s, d_s, norm_ffn3, whi, wlo, br, l)
        pos_flat, tile_expert, next_expert, n_used = _routing_plan(eid, rank, counts, n_tiles)
        sorted_token = _invert(pos_flat, t_all, n_tiles * TE)
        y = _experts(tile_expert, next_expert, n_used, sorted_token, xn, w_exp_gate, w_exp_up, w_exp_down, l)
        final = l == depth - 1
        x_p, x_s = _combine(pos_flat, x_mid, jnp.transpose(wts), norm_final.reshape(1, dm), y, final, n_p)

    y_prompt = x_p.reshape(nb, seq, dm)
    y_sample = jnp.transpose(x_s.reshape(n_steps, n_seq, dm), (1, 0, 2))
    return (y_prompt, y_sample,
            jnp.stack(conv_p), jnp.stack(re_p), jnp.stack(im_p),
            jnp.stack(conv_s), jnp.stack(re_s), jnp.stack(im_s))
```

```python
import functools

import jax
import jax.numpy as jnp
from jax import lax
from jax.experimental import pallas as pl
from jax.experimental.pallas import tpu as pltpu

F32 = jnp.float32
BF16 = jnp.bfloat16
EPS = 1e-6

VMEM_LIMIT_BYTES = 56 * 1024 * 1024
SUBLANES = 8
LANES = 128

D_MODEL = 2048
WC = 1024
WS = 1024
SSM_H = 16
SSM_G = 64
SSM_P = 64
N_PROJ = 3 * WC + WS + 2 * D_MODEL
N_GROUPS = 4
EXPERTS_PER_GROUP = 8
N_EXPERTS = N_GROUPS * EXPERTS_PER_GROUP
TOP_K = 2
D_EXPERT = 512

GROUPS_PER_SLICE = LANES // SSM_H
N_SLICES = SSM_G // GROUPS_PER_SLICE
SLICE_STATES = GROUPS_PER_SLICE * SSM_P
SLICE_COLS = 2 * SLICE_STATES

TM_PROJ = 1024
TN_PROJ = 512
TM_CONV = 512
TN_CONV = 512
TM_SSM = 1024
TM_MERGE = 1024
TJ_MERGE = 512
TM_ROUTE = 512
TE = 256
WEIGHT_DMA_PRIORITY = 1
TM_COMBINE = 256


def _cparams(*sem):
    return pltpu.CompilerParams(dimension_semantics=sem, vmem_limit_bytes=VMEM_LIMIT_BYTES)


def _rmsnorm(x, g):
    ms = jnp.mean(x * x, axis=-1, keepdims=True)
    return x * lax.rsqrt(ms + EPS) * g


SLAB_ROWS = D_MODEL // LANES


def _store_slabs(ref, x, row0=0):
    n = x.shape[0]
    for j in range(SLAB_ROWS):
        ref[pl.ds(row0 * SLAB_ROWS + j, n, stride=SLAB_ROWS), :] = x[:, j * LANES:(j + 1) * LANES]


def _load_slabs(ref, row0, n):
    parts = [ref[pl.ds(row0 * SLAB_ROWS + j, n, stride=SLAB_ROWS), :] for j in range(SLAB_ROWS)]
    return jnp.concatenate(parts, axis=1)


def _in_proj_kernel(x_ref, g_ref, w_ref, p_ref, u_ref, xn_ref, *, u_first, u_last):
    j = pl.program_id(1)

    @pl.when(j == 0)
    def _():
        xn_ref[...] = _rmsnorm(x_ref[...], g_ref[...]).astype(BF16)

    acc = jnp.dot(xn_ref[...], w_ref[...].astype(BF16), preferred_element_type=F32)
    p_ref[...] = acc

    @pl.when((j >= u_first) & (j <= u_last))
    def _():
        u_ref[...] = acc


def _in_proj(x, norm_g, w_in, layer, u_shape, u_index, name):
    n_row_blks = x.shape[0] // TM_PROJ
    u_first = (3 * WC) // TN_PROJ
    u_last = (3 * WC + WS) // TN_PROJ - 1
    kern = functools.partial(_in_proj_kernel, u_first=u_first, u_last=u_last)
    return pl.pallas_call(
        kern,
        grid=(n_row_blks, N_PROJ // TN_PROJ),
        in_specs=[
            pl.BlockSpec((TM_PROJ, D_MODEL), lambda i, j: (i, 0)),
            pl.BlockSpec((None, 1, D_MODEL), lambda i, j: (layer, 0, 0)),
            pl.BlockSpec((None, D_MODEL, TN_PROJ), lambda i, j: (layer, 0, j)),
        ],
        out_specs=[
            pl.BlockSpec((TM_PROJ, TN_PROJ), lambda i, j: (i, j)),
            pl.BlockSpec((TM_PROJ, TN_PROJ), functools.partial(u_index, u_first=u_first)),
        ],
        out_shape=[
            jax.ShapeDtypeStruct((n_row_blks * TM_PROJ, N_PROJ), F32),
            jax.ShapeDtypeStruct(u_shape, F32),
        ],
        scratch_shapes=[pltpu.VMEM((TM_PROJ, D_MODEL), BF16)],
        compiler_params=_cparams("arbitrary", "arbitrary"),
        name=name,
    )(x, norm_g, w_in)


def _conv_kernel(b_ref, c_ref, h_ref, init_ref, cw_ref, w_ref, ya_ref, st_ref, bc_ref, carry_ref,
                 *, shift, tiles_per_seq):
    i = pl.program_id(0)
    j = pl.program_id(1)
    tm = b_ref.shape[0]
    nc = carry_ref.shape[0]

    @pl.when(j == 0)
    def _():
        @pl.when(i % tiles_per_seq == 0)
        def _():
            carry_ref[...] = init_ref[...]

        v = c_ref[...] * h_ref[...]
        ext = jnp.concatenate([carry_ref[...], v], axis=0)
        v1 = ext[nc - shift:nc - shift + tm]
        v2 = ext[nc - 2 * shift:nc - 2 * shift + tm]
        cw = cw_ref[...]
        conv = cw[0:1] * v2 + cw[1:2] * v1 + cw[2:3] * v
        bc_ref[...] = (b_ref[...] * conv).astype(BF16)
        carry_ref[...] = v[tm - nc:]
        st_ref[...] = v[tm - nc:]

    ya_ref[...] = jnp.dot(bc_ref[...], w_ref[...].astype(BF16), preferred_element_type=F32)


def _conv_branch(proj, init, conv_w, w_out_conv, layer, tm, shift, tiles_per_seq, name):
    n_rows = proj.shape[0]
    nc = init.shape[1]
    n_seq = init.shape[0]
    kern = functools.partial(_conv_kernel, shift=shift, tiles_per_seq=tiles_per_seq)
    return pl.pallas_call(
        kern,
        grid=(n_rows // tm, D_MODEL // TN_CONV),
        in_specs=[
            pl.BlockSpec((tm, WC), lambda i, j: (i, 0)),
            pl.BlockSpec((tm, WC), lambda i, j: (i, 1)),
            pl.BlockSpec((tm, WC), lambda i, j: (i, 2)),
            pl.BlockSpec((None, nc, WC), lambda i, j: (i // tiles_per_seq, 0, 0)),
            pl.BlockSpec((None, 3, WC), lambda i, j: (layer, 0, 0)),
            pl.BlockSpec((None, WC, TN_CONV), lambda i, j: (layer, 0, j)),
        ],
        out_specs=[
            pl.BlockSpec((tm, TN_CONV), lambda i, j: (i, j)),
            pl.BlockSpec((None, nc, WC), lambda i, j: (i // tiles_per_seq, 0, 0)),
        ],
        out_shape=[
            jax.ShapeDtypeStruct((n_rows, D_MODEL), F32),
            jax.ShapeDtypeStruct((n_seq, nc, WC), F32),
        ],
        scratch_shapes=[pltpu.VMEM((tm, WC), BF16), pltpu.VMEM((nc, WC), F32)],
        compiler_params=_cparams("arbitrary", "arbitrary"),
        name=name,
    )(proj, proj, proj, init, conv_w, w_out_conv)


def _disc_kernel(lr_ref, li_ref, ls_ref, br_ref, bi_ref, lb_ref, a2_ref, bb_ref, ab_ref):
    lr = lr_ref[...]
    li = li_ref[...]
    dt = jnp.exp(ls_ref[...])
    mag = jnp.exp(lr * dt)
    lbr = mag * jnp.cos(li * dt)
    lbi = mag * jnp.sin(li * dt)
    nr, ni = lbr - 1.0, lbi
    den = lr * lr + li * li
    fr = (nr * lr + ni * li) / den
    fi = (ni * lr - nr * li) / den
    lb_ref[0] = lbr
    lb_ref[1] = lbi
    a2_ref[0] = lbr * lbr - lbi * lbi
    a2_ref[1] = 2.0 * lbr * lbi
    frt = jnp.concatenate([fr] * SSM_H, axis=0)
    fit = jnp.concatenate([fi] * SSM_H, axis=0)
    lbrt = jnp.concatenate([lbr] * SSM_H, axis=0)
    lbit = jnp.concatenate([lbi] * SSM_H, axis=0)
    br = br_ref[...]
    bi = bi_ref[...]
    bbr = frt * br - fit * bi
    bbi = frt * bi + fit * br
    bb_ref[0] = bbr
    bb_ref[1] = bbi
    ab_ref[0] = lbrt * bbr - lbit * bbi
    ab_ref[1] = lbrt * bbi + lbit * bbr


def _discretize(lam_re, lam_im, log_step, b_re, b_im):
    g, p = lam_re.shape
    hg = SSM_H * g
    brt = jnp.transpose(b_re, (2, 0, 1)).reshape(hg, p)
    bit = jnp.transpose(b_im, (2, 0, 1)).reshape(hg, p)
    return pl.pallas_call(
        _disc_kernel,
        out_shape=[
            jax.ShapeDtypeStruct((2, g, p), F32),
            jax.ShapeDtypeStruct((2, g, p), F32),
            jax.ShapeDtypeStruct((2, hg, p), F32),
            jax.ShapeDtypeStruct((2, hg, p), F32),
        ],
        name="s5_discretize",
    )(lam_re, lam_im, log_step.reshape(g, 1), brt, bit)


def _block_diag_in(m):
    m = m.reshape(SSM_H, N_SLICES, GROUPS_PER_SLICE, SSM_P)
    eye = jnp.eye(GROUPS_PER_SLICE, dtype=m.dtype)
    out = jnp.einsum('hkgp,gq->kghqp', m, eye)
    return out.reshape(N_SLICES, LANES, SLICE_STATES)


def _block_diag_out(c):
    c = c.reshape(N_SLICES, GROUPS_PER_SLICE, SSM_H, SSM_P)
    eye = jnp.eye(GROUPS_PER_SLICE, dtype=c.dtype)
    out = jnp.einsum('kghp,gq->kgpqh', c, eye)
    return out.reshape(N_SLICES, SLICE_STATES, LANES)


def _slice_rows(a):
    a = a.reshape(2, N_SLICES, SLICE_STATES)
    a = jnp.transpose(a, (1, 0, 2)).reshape(N_SLICES, 1, SLICE_COLS)
    return jnp.broadcast_to(a, (N_SLICES, SUBLANES, SLICE_COLS))


def _gelu_out(x_state, u, c_ref, d_ref):
    y = jnp.dot(x_state.astype(BF16), c_ref[...], preferred_element_type=F32)
    s = y + d_ref[...] * u
    return jax.nn.gelu(s).astype(BF16)


def _ssm_prompt_kernel(u_ref, wb_ref, a_ref, c_ref, d_ref, z_ref, hs_ref, v_ref, st_ref, up_ref, *, nb):
    tb = pl.program_id(1)
    tm = u_ref.shape[0]
    half = SLICE_STATES

    @pl.when(tb == 0)
    def _():
        st_ref[...] = jnp.zeros_like(st_ref)
        up_ref[...] = jnp.zeros_like(up_ref)

    u = u_ref[...]
    ext = jnp.concatenate([up_ref[...], u], axis=0)
    u_prev = ext[SUBLANES - nb:SUBLANES - nb + tm]
    lhs = jnp.concatenate([u, u_prev], axis=1).astype(BF16)
    v_ref[...] = jnp.dot(lhs, wb_ref[...], preferred_element_type=F32)
    up_ref[...] = u[tm - SUBLANES:]

    a_re = a_ref[:, :half]
    a_im = a_ref[:, half:]

    def body(s, carry):
        sre, sim = carry
        r0 = pl.multiple_of(s * SUBLANES, SUBLANES)
        vre = v_ref[pl.ds(r0, SUBLANES), :half]
        vim = v_ref[pl.ds(r0, SUBLANES), half:]
        nre = a_re * sre - a_im * sim + vre
        nim = a_re * sim + a_im * sre + vim
        v_ref[pl.ds(r0, SUBLANES), :half] = nre
        v_ref[pl.ds(r0, SUBLANES), half:] = nim
        return nre, nim

    sre, sim = lax.fori_loop(0, tm // SUBLANES, body, (st_ref[:, :half], st_ref[:, half:]))
    st_ref[:, :half] = sre
    st_ref[:, half:] = sim
    hs_ref[:, :half] = sre
    hs_ref[:, half:] = sim
    z_ref[...] = _gelu_out(v_ref[...], u, c_ref, d_ref)


def _ssm_prompt(u_tm, wb, a2, cmat, dvec, nb):
    n_rows = u_tm.shape[0]
    kern = functools.partial(_ssm_prompt_kernel, nb=nb)
    return pl.pallas_call(
        kern,
        grid=(N_SLICES, n_rows // TM_SSM),
        in_specs=[
            pl.BlockSpec((TM_SSM, LANES), lambda k, t: (t, k)),
            pl.BlockSpec((None, 2 * LANES, SLICE_COLS), lambda k, t: (k, 0, 0)),
            pl.BlockSpec((None, SUBLANES, SLICE_COLS), lambda k, t: (k, 0, 0)),
            pl.BlockSpec((None, SLICE_COLS, LANES), lambda k, t: (k, 0, 0)),
            pl.BlockSpec((None, 1, LANES), lambda k, t: (k, 0, 0)),
        ],
        out_specs=[
            pl.BlockSpec((TM_SSM, LANES), lambda k, t: (t, k)),
            pl.BlockSpec((None, SUBLANES, SLICE_COLS), lambda k, t: (k, 0, 0)),
        ],
        out_shape=[
            jax.ShapeDtypeStruct((n_rows, WS), BF16),
            jax.ShapeDtypeStruct((N_SLICES, SUBLANES, SLICE_COLS), F32),
        ],
        scratch_shapes=[
            pltpu.VMEM((TM_SSM, SLICE_COLS), F32),
            pltpu.VMEM((SUBLANES, SLICE_COLS), F32),
            pltpu.VMEM((SUBLANES, LANES), F32),
        ],
        compiler_params=_cparams("arbitrary", "arbitrary"),
        name="s5_prompt",
    )(u_tm, wb, a2, cmat, dvec)


def _ssm_sample_kernel(u_ref, h0_ref, wb_ref, a_ref, c_ref, d_ref, z_ref, hs_ref, v_ref, *, n_seq, n_steps):
    half = SLICE_STATES
    u = u_ref[...]
    v_ref[...] = jnp.dot(u.astype(BF16), wb_ref[...], preferred_element_type=F32)
    a_re = a_ref[:, :half]
    a_im = a_ref[:, half:]

    def body(q, _):
        n0 = pl.multiple_of(q * SUBLANES, SUBLANES)
        sre = h0_ref[pl.ds(n0, SUBLANES), :half]
        sim = h0_ref[pl.ds(n0, SUBLANES), half:]
        for t in range(n_steps):
            r0 = pl.multiple_of(t * n_seq + n0, SUBLANES)
            vre = v_ref[pl.ds(r0, SUBLANES), :half]
            vim = v_ref[pl.ds(r0, SUBLANES), half:]
            nre = a_re * sre - a_im * sim + vre
            nim = a_re * sim + a_im * sre + vim
            v_ref[pl.ds(r0, SUBLANES), :half] = nre
            v_ref[pl.ds(r0, SUBLANES), half:] = nim
            sre, sim = nre, nim
        hs_ref[pl.ds(n0, SUBLANES), :half] = sre
        hs_ref[pl.ds(n0, SUBLANES), half:] = sim
        return 0

    lax.fori_loop(0, n_seq // SUBLANES, body, 0)
    z_ref[...] = _gelu_out(v_ref[...], u, c_ref, d_ref)


def _ssm_sample(u_s, h0, wb1, a1, cmat, dvec, n_seq, n_steps):
    n_rows = u_s.shape[0]
    kern = functools.partial(_ssm_sample_kernel, n_seq=n_seq, n_steps=n_steps)
    return pl.pallas_call(
        kern,
        grid=(N_SLICES,),
        in_specs=[
            pl.BlockSpec((n_rows, LANES), lambda k: (0, k)),
            pl.BlockSpec((n_seq, SLICE_COLS), lambda k: (0, k)),
            pl.BlockSpec((None, LANES, SLICE_COLS), lambda k: (k, 0, 0)),
            pl.BlockSpec((None, SUBLANES, SLICE_COLS), lambda k: (k, 0, 0)),
            pl.BlockSpec((None, SLICE_COLS, LANES), lambda k: (k, 0, 0)),
            pl.BlockSpec((None, 1, LANES), lambda k: (k, 0, 0)),
        ],
        out_specs=[
            pl.BlockSpec((n_rows, LANES), lambda k: (0, k)),
            pl.BlockSpec((n_seq, SLICE_COLS), lambda k: (0, k)),
        ],
        out_shape=[
            jax.ShapeDtypeStruct((n_rows, WS), BF16),
            jax.ShapeDtypeStruct((n_seq, N_SLICES * SLICE_COLS), F32),
        ],
        scratch_shapes=[pltpu.VMEM((n_rows, SLICE_COLS), F32)],
        compiler_params=_cparams("arbitrary"),
        name="s5_sample",
    )(u_s, h0, wb1, a1, cmat, dvec)


def _merge_kernel(z_ref, ya_ref, pa_ref, pb_ref, bga_ref, bgb_ref, wga_ref, wgb_ref, wo_ref, o_ref):
    j = pl.program_id(1)
    z = z_ref[...]
    glu_a = jnp.dot(z, wga_ref[...].astype(BF16), preferred_element_type=F32)
    glu_b = jnp.dot(z, wgb_ref[...].astype(BF16), preferred_element_type=F32)
    yb = glu_a * jax.nn.sigmoid(glu_b)
    g1 = jax.nn.sigmoid(pa_ref[...] + bga_ref[...])
    g2 = jax.nn.sigmoid(pb_ref[...] + bgb_ref[...])
    merged = (g1 * ya_ref[...] + g2 * yb).astype(BF16)
    part = jnp.dot(merged, wo_ref[...].astype(BF16), preferred_element_type=F32)

    @pl.when(j == 0)
    def _():
        o_ref[...] = part

    @pl.when(j > 0)
    def _():
        o_ref[...] += part


def _merge(z, z_index, ya, proj, b_gate, w_glu, w_o, layer, name):
    n_rows = ya.shape[0]
    nj = D_MODEL // TJ_MERGE
    ga0 = (3 * WC + WS) // TJ_MERGE
    gb0 = ga0 + nj
    in_specs = [
        pl.BlockSpec((TM_MERGE, WS), z_index),
        pl.BlockSpec((TM_MERGE, TJ_MERGE), lambda i, j: (i, j)),
        pl.BlockSpec((TM_MERGE, TJ_MERGE), lambda i, j: (i, ga0 + j)),
        pl.BlockSpec((TM_MERGE, TJ_MERGE), lambda i, j: (i, gb0 + j)),
        pl.BlockSpec((None, 1, TJ_MERGE), lambda i, j: (layer, 0, j)),
        pl.BlockSpec((None, 1, TJ_MERGE), lambda i, j: (layer, 0, nj + j)),
        pl.BlockSpec((None, WS, TJ_MERGE), lambda i, j: (layer, 0, j)),
        pl.BlockSpec((None, WS, TJ_MERGE), lambda i, j: (layer, 0, nj + j)),
        pl.BlockSpec((None, TJ_MERGE, D_MODEL), lambda i, j: (layer, j, 0)),
    ]
    return pl.pallas_call(
        _merge_kernel,
        grid=(n_rows // TM_MERGE, nj),
        in_specs=in_specs,
        out_specs=pl.BlockSpec((TM_MERGE, D_MODEL), lambda i, j: (i, 0)),
        out_shape=jax.ShapeDtypeStruct((n_rows, D_MODEL), F32),
        compiler_params=_cparams("arbitrary", "arbitrary"),
        name=name,
    )(z, ya, proj, proj, b_gate, b_gate, w_glu, w_glu, w_o)


def _router_kernel(xp_ref, dp_ref, xs_ref, ds_ref, g_ref, whi_ref, wlo_ref, br_ref,
                   xmid_ref, xn_ref, eid_ref, wts_ref, rank_ref, cnt_ref, run_ref, *, np_blocks):
    i = pl.program_id(0)
    tm = xp_ref.shape[0]

    @pl.when(i == 0)
    def _():
        run_ref[...] = jnp.zeros_like(run_ref)

    @pl.when(i < np_blocks)
    def _():
        xmid_ref[...] = xp_ref[...] + dp_ref[...]

    @pl.when(i >= np_blocks)
    def _():
        xmid_ref[...] = xs_ref[...] + ds_ref[...]

    xm = xmid_ref[...]
    xn = _rmsnorm(xm, g_ref[...])
    _store_slabs(xn_ref, xn)
    hi = xn.astype(BF16)
    lo = (xn - hi.astype(F32)).astype(BF16)
    nt = (((1,), (1,)), ((), ()))
    whi = whi_ref[...]
    wlo = wlo_ref[...]
    lt = (lax.dot_general(whi, hi, nt, preferred_element_type=F32)
          + lax.dot_general(wlo, hi, nt, preferred_element_type=F32)
          + lax.dot_general(whi, lo, nt, preferred_element_type=F32))
    lt = lt + br_ref[...]

    neg = jnp.float32(-jnp.inf)
    row8 = lax.broadcasted_iota(jnp.int32, (SUBLANES, tm), 0)
    lg = jnp.where(row8 < N_GROUPS, lt[0:SUBLANES], neg)
    gmax = jnp.max(lg, axis=0, keepdims=True)
    gi = jnp.min(jnp.where(lg == gmax, row8, SUBLANES), axis=0, keepdims=True)
    gp = 1.0 / jnp.sum(jnp.exp(lg - gmax), axis=0, keepdims=True)

    sel = lt[SUBLANES:2 * SUBLANES]
    for g in range(1, N_GROUPS):
        sel = jnp.where(gi == g, lt[(g + 1) * SUBLANES:(g + 2) * SUBLANES], sel)
    m1 = jnp.max(sel, axis=0, keepdims=True)
    i1 = jnp.min(jnp.where(sel == m1, row8, SUBLANES), axis=0, keepdims=True)
    sel2 = jnp.where(row8 == i1, neg, sel)
    m2 = jnp.max(sel2, axis=0, keepdims=True)
    i2 = jnp.min(jnp.where(sel2 == m2, row8, SUBLANES), axis=0, keepdims=True)
    e = jnp.exp(m2 - m1)
    w1 = gp / (1.0 + e)
    w2 = gp * e / (1.0 + e)
    e1 = gi * EXPERTS_PER_GROUP + i1
    e2 = gi * EXPERTS_PER_GROUP + i2
    eid_ref[0:1, :] = e1
    eid_ref[1:2, :] = e2
    wts_ref[0:1, :] = w1
    wts_ref[1:2, :] = w2

    erow = lax.broadcasted_iota(jnp.int32, (N_EXPERTS, tm), 0)
    oh1 = jnp.where(erow == e1, 1.0, 0.0)
    oh2 = jnp.where(erow == e2, 1.0, 0.0)
    oh = jnp.concatenate([oh1, oh2], axis=0).astype(BF16)
    tri = jnp.where(lax.broadcasted_iota(jnp.int32, (tm, tm), 0) < lax.broadcasted_iota(jnp.int32, (tm, tm), 1),
                    1.0, 0.0).astype(BF16)
    pref = jnp.dot(oh, tri, preferred_element_type=F32)
    p1 = pref[:N_EXPERTS]
    p2 = pref[N_EXPERTS:]
    tot1 = jnp.sum(oh1, axis=1, keepdims=True)
    tot2 = jnp.sum(oh2, axis=1, keepdims=True)
    run = run_ref[...]
    r1 = jnp.sum(oh1 * (run + p1), axis=0, keepdims=True)
    r2 = jnp.sum(oh2 * (run + tot1 + p2), axis=0, keepdims=True)
    rank_ref[0:1, :] = r1.astype(jnp.int32)
    rank_ref[1:2, :] = r2.astype(jnp.int32)
    run = run + tot1 + tot2
    run_ref[...] = run
    cnt_ref[...] = run


def _router(x_p, d_p, x_s, d_s, norm_g, whi, wlo, br, layer):
    np_blocks = x_p.shape[0] // TM_ROUTE
    t = x_p.shape[0] + x_s.shape[0]
    row = lambda i: (i, 0)
    row_p = lambda i: (jnp.minimum(i, np_blocks - 1), 0)
    row_s = lambda i: (jnp.maximum(i - np_blocks, 0), 0)
    col = lambda i: (0, i)
    fixed = lambda i: (0, 0)
    return pl.pallas_call(
        functools.partial(_router_kernel, np_blocks=np_blocks),
        grid=(t // TM_ROUTE,),
        in_specs=[
            pl.BlockSpec((TM_ROUTE, D_MODEL), row_p),
            pl.BlockSpec((TM_ROUTE, D_MODEL), row_p),
            pl.BlockSpec((TM_ROUTE, D_MODEL), row_s),
            pl.BlockSpec((TM_ROUTE, D_MODEL), row_s),
            pl.BlockSpec((None, 1, D_MODEL), lambda i: (layer, 0, 0)),
            pl.BlockSpec((LANES, D_MODEL), fixed),
            pl.BlockSpec((LANES, D_MODEL), fixed),
            pl.BlockSpec((LANES, 1), fixed),
        ],
        out_specs=[
            pl.BlockSpec((TM_ROUTE, D_MODEL), row),
            pl.BlockSpec((TM_ROUTE * SLAB_ROWS, LANES), row),
            pl.BlockSpec((TOP_K, TM_ROUTE), col),
            pl.BlockSpec((TOP_K, TM_ROUTE), col),
            pl.BlockSpec((TOP_K, TM_ROUTE), col),
            pl.BlockSpec((N_EXPERTS, 1), fixed),
        ],
        out_shape=[
            jax.ShapeDtypeStruct((t, D_MODEL), F32),
            jax.ShapeDtypeStruct((t * SLAB_ROWS, LANES), F32),
            jax.ShapeDtypeStruct((TOP_K, t), jnp.int32),
            jax.ShapeDtypeStruct((TOP_K, t), F32),
            jax.ShapeDtypeStruct((TOP_K, t), jnp.int32),
            jax.ShapeDtypeStruct((N_EXPERTS, 1), F32),
        ],
        scratch_shapes=[pltpu.VMEM((N_EXPERTS, 1), F32)],
        compiler_params=_cparams("arbitrary"),
        name="moe_router",
    )(x_p, d_p, x_s, d_s, norm_g, whi, wlo, br)


def _invert_kernel(pos_ref, sorted_ref, *, n_tokens):
    def clear(p, _):
        sorted_ref[p] = 0
        return 0

    lax.fori_loop(0, sorted_ref.shape[0], clear, 0, unroll=8)

    def place(t, _):
        for k in range(TOP_K):
            sorted_ref[pos_ref[k * n_tokens + t]] = t
        return 0

    lax.fori_loop(0, n_tokens, place, 0, unroll=8)


def _invert(pos_flat, n_tokens, n_sorted):
    return pl.pallas_call(
        functools.partial(_invert_kernel, n_tokens=n_tokens),
        in_specs=[pl.BlockSpec(memory_space=pltpu.SMEM)],
        out_specs=pl.BlockSpec(memory_space=pltpu.SMEM),
        out_shape=jax.ShapeDtypeStruct((n_sorted,), jnp.int32),
        name="moe_invert",
    )(pos_flat)


def _experts_kernel(te_ref, nx_ref, nu_ref, st_ref, xn_hbm, wg_hbm, wu_hbm, wd_hbm, y_ref,
                    wgf, wuf, wdf, wgs, wus, wds, xbuf, xsems, wsems, *, layer):
    i = pl.program_id(0)
    slot = i % 2
    slab_tile = TE * SLAB_ROWS

    def gather(tile, s):
        base = tile * TE

        def body(r, _):
            src = pl.multiple_of(st_ref[base + r] * SLAB_ROWS, SLAB_ROWS)
            dst = pl.multiple_of((s * TE + r) * SLAB_ROWS, SLAB_ROWS)
            pltpu.make_async_copy(xn_hbm.at[pl.ds(src, SLAB_ROWS), :],
                                  xbuf.at[pl.ds(dst, SLAB_ROWS), :], xsems.at[s]).start()
            return 0

        lax.fori_loop(0, TE, body, 0, unroll=8)

    def weight_copies(e):
        return (pltpu.make_async_copy(wg_hbm.at[layer, e], wgf, wsems.at[0]),
                pltpu.make_async_copy(wu_hbm.at[layer, e], wuf, wsems.at[1]),
                pltpu.make_async_copy(wd_hbm.at[layer, e], wdf, wsems.at[2]))

    @pl.when(i == 0)
    def _():
        gather(0, 0)
        for cp in weight_copies(te_ref[0]):
            cp.start(priority=WEIGHT_DMA_PRIORITY)

    @pl.when(i + 1 < nu_ref[0])
    def _():
        gather(i + 1, 1 - slot)

    @pl.when(i < nu_ref[0])
    def _():
        prev = te_ref[jnp.maximum(i - 1, 0)]

        @pl.when((i == 0) | (te_ref[i] != prev))
        def _():
            for cp in weight_copies(te_ref[i]):
                cp.wait()
            wgs[...] = wgf[...].astype(BF16)
            wus[...] = wuf[...].astype(BF16)
            wds[...] = wdf[...].astype(BF16)

            @pl.when(nx_ref[i] >= 0)
            def _():
                for cp in weight_copies(nx_ref[i]):
                    cp.start(priority=WEIGHT_DMA_PRIORITY)

        x0 = pl.multiple_of(slot * slab_tile, slab_tile)
        pltpu.make_async_copy(xn_hbm.at[pl.ds(0, slab_tile), :], xbuf.at[pl.ds(x0, slab_tile), :],
                              xsems.at[slot]).wait()
        x = _load_slabs(xbuf, slot * TE, TE).astype(BF16)
        hg = jnp.dot(x, wgs[...], preferred_element_type=F32)
        hu = jnp.dot(x, wus[...], preferred_element_type=F32)
        h = (hg * jax.nn.sigmoid(hg) * hu).astype(BF16)
        _store_slabs(y_ref, jnp.dot(h, wds[...], preferred_element_type=F32))

    @pl.when(i >= nu_ref[0])
    def _():
        y_ref[...] = jnp.zeros_like(y_ref)


def _experts(tile_expert, next_expert, n_used, sorted_token, xn_slabs, w_gate, w_up, w_down, layer):
    n_tiles = sorted_token.shape[0] // TE
    hbm = pl.BlockSpec(memory_space=pl.ANY)
    return pl.pallas_call(
        functools.partial(_experts_kernel, layer=layer),
        grid_spec=pltpu.PrefetchScalarGridSpec(
            num_scalar_prefetch=4,
            grid=(n_tiles,),
            in_specs=[hbm, hbm, hbm, hbm],
            out_specs=pl.BlockSpec((TE * SLAB_ROWS, LANES), lambda i, te, nx, nu, st: (i, 0)),
            scratch_shapes=[
                pltpu.VMEM((D_MODEL, D_EXPERT), F32),
                pltpu.VMEM((D_MODEL, D_EXPERT), F32),
                pltpu.VMEM((D_EXPERT, D_MODEL), F32),
                pltpu.VMEM((D_MODEL, D_EXPERT), BF16),
                pltpu.VMEM((D_MODEL, D_EXPERT), BF16),
                pltpu.VMEM((D_EXPERT, D_MODEL), BF16),
                pltpu.VMEM((2 * TE * SLAB_ROWS, LANES), F32),
                pltpu.SemaphoreType.DMA((2,)),
                pltpu.SemaphoreType.DMA((3,)),
            ],
        ),
        out_shape=jax.ShapeDtypeStruct((n_tiles * TE * SLAB_ROWS, LANES), F32),
        compiler_params=_cparams("arbitrary"),
        name="moe_experts",
    )(tile_expert, next_expert, n_used, sorted_token, xn_slabs, w_gate, w_up, w_down)


def _combine_kernel(pos_ref, x_ref, w_ref, gf_ref, y_hbm, op_ref, os_ref, ybuf, sems,
                    *, n_tokens, final, np_blocks):
    i = pl.program_id(0)
    n = pl.num_programs(0)
    slot = i % 2
    tm = x_ref.shape[0]

    def issue(step, s):
        base = step * tm

        def body(r, _):
            for k in range(TOP_K):
                src = pl.multiple_of(pos_ref[k * n_tokens + base + r] * SLAB_ROWS, SLAB_ROWS)
                dst = pl.multiple_of(((s * TOP_K + k) * tm + r) * SLAB_ROWS, SLAB_ROWS)
                pltpu.make_async_copy(y_hbm.at[pl.ds(src, SLAB_ROWS), :],
                                      ybuf.at[pl.ds(dst, SLAB_ROWS), :], sems.at[s]).start()
            return 0

        lax.fori_loop(0, tm, body, 0, unroll=8)

    @pl.when(i == 0)
    def _():
        issue(0, 0)

    @pl.when(i + 1 < n)
    def _():
        issue(i + 1, 1 - slot)

    slab_slot = TOP_K * tm * SLAB_ROWS
    y0 = pl.multiple_of(slot * slab_slot, slab_slot)
    pltpu.make_async_copy(y_hbm.at[pl.ds(0, slab_slot), :], ybuf.at[pl.ds(y0, slab_slot), :], sems.at[slot]).wait()

    w = w_ref[...]
    out = (x_ref[...] + w[:, 0:1] * _load_slabs(ybuf, (slot * TOP_K) * tm, tm)
           + w[:, 1:2] * _load_slabs(ybuf, (slot * TOP_K + 1) * tm, tm))
    if final:
        out = _rmsnorm(out, gf_ref[...])

    @pl.when(i < np_blocks)
    def _():
        op_ref[...] = out

    @pl.when(i >= np_blocks)
    def _():
        os_ref[...] = out


def _combine(pos_flat, x_mid, w_t, g_final, y, final, n_prompt):
    t = x_mid.shape[0]
    np_blocks = n_prompt // TM_COMBINE
    row = lambda i, pos: (i, 0)
    out_specs = [
        pl.BlockSpec((TM_COMBINE, D_MODEL), lambda i, pos: (jnp.minimum(i, np_blocks - 1), 0)),
        pl.BlockSpec((TM_COMBINE, D_MODEL), lambda i, pos: (jnp.maximum(i - np_blocks, 0), 0)),
    ]
    out_shape = [
        jax.ShapeDtypeStruct((n_prompt, D_MODEL), F32),
        jax.ShapeDtypeStruct((t - n_prompt, D_MODEL), F32),
    ]
    return pl.pallas_call(
        functools.partial(_combine_kernel, n_tokens=t, final=final, np_blocks=np_blocks),
        grid_spec=pltpu.PrefetchScalarGridSpec(
            num_scalar_prefetch=1,
            grid=(t // TM_COMBINE,),
            in_specs=[
                pl.BlockSpec((TM_COMBINE, D_MODEL), row),
                pl.BlockSpec((TM_COMBINE, TOP_K), row),
                pl.BlockSpec((1, D_MODEL), lambda i, pos: (0, 0)),
                pl.BlockSpec(memory_space=pl.ANY),
            ],
            out_specs=out_specs,
            scratch_shapes=[
                pltpu.VMEM((2 * TOP_K * TM_COMBINE * SLAB_ROWS, LANES), F32),
                pltpu.SemaphoreType.DMA((2,)),
            ],
        ),
        out_shape=out_shape,
        compiler_params=_cparams("arbitrary"),
        name="moe_combine_final" if final else "moe_combine",
    )(pos_flat, x_mid, w_t, g_final, y)


def _routing_plan(eid, rank, counts, n_tiles):
    cnt = counts[:, 0].astype(jnp.int32)
    padded = ((cnt + TE - 1) // TE) * TE
    ends = jnp.cumsum(padded)
    off = ends - padded
    n_used = (ends[-1] // TE).astype(jnp.int32)
    experts = jnp.arange(N_EXPERTS, dtype=jnp.int32)
    pos = rank + jnp.sum(jnp.where(eid[..., None] == experts, off, 0), axis=-1)
    tile_start = jnp.arange(n_tiles, dtype=jnp.int32) * TE
    te = jnp.sum((tile_start[:, None] >= ends[None, :]).astype(jnp.int32), axis=1)
    last_expert = jnp.max(jnp.where(cnt > 0, experts, 0))
    te = jnp.where(tile_start < ends[-1], te, last_expert)
    later = (experts[None, :] > te[:, None]) & (cnt[None, :] > 0)
    nxt = jnp.min(jnp.where(later, experts[None, :], N_EXPERTS), axis=1)
    nxt = jnp.where(nxt < N_EXPERTS, nxt, -1).astype(jnp.int32)
    return pos.reshape(-1), te, nxt, n_used.reshape(1)


def kernel(x_prompt, x_sample, state_conv, state_ssm_re, state_ssm_im, norm_mix, w_in, b_gate, conv_w, w_out_conv, ssm_lambda_re, ssm_lambda_im, ssm_log_step, ssm_b_re, ssm_b_im, ssm_c_re, ssm_c_im, ssm_d, w_glu, w_o, norm_ffn, w_router_group, b_router_group, w_router_expert, b_router_expert, w_exp_gate, w_exp_up, w_exp_down, norm_final):
    nb, seq, dm = x_prompt.shape
    n_seq, n_steps, _ = x_sample.shape
    depth = w_in.shape[0]
    assert dm == D_MODEL and w_in.shape[2] == N_PROJ and w_exp_gate.shape[1:] == (N_EXPERTS, D_MODEL, D_EXPERT)
    assert ssm_lambda_re.shape[1:] == (SSM_G, SSM_P) and ssm_b_re.shape[3] == SSM_H
    assert 2 * nb == SUBLANES and n_seq % SUBLANES == 0
    n_p = nb * seq
    n_s = n_seq * n_steps
    t_all = n_p + n_s
    assert seq % TM_PROJ == 0 and n_s == TM_PROJ and n_s == TM_MERGE and seq % TM_CONV == 0
    assert n_p % TM_ROUTE == 0 and n_s % TM_ROUTE == 0 and n_p % TM_COMBINE == 0 and n_s % TM_COMBINE == 0
    n_tiles = (TOP_K * t_all + N_EXPERTS * (TE - 1)) // TE

    x_p = x_prompt.reshape(n_p, dm)
    x_s = jnp.transpose(x_sample, (1, 0, 2)).reshape(n_s, dm)
    norm_mix3 = norm_mix.reshape(depth, 1, dm)
    norm_ffn3 = norm_ffn.reshape(depth, 1, dm)
    b_gate3 = b_gate.reshape(depth, 1, 2 * dm)
    zero_conv = jnp.zeros((nb, SUBLANES, WC), F32)
    seq_blks = seq // TM_PROJ

    def u_index_prompt(i, j, u_first):
        return (i % seq_blks, (i // seq_blks) * (WS // TN_PROJ) + jnp.clip(j - u_first, 0, WS // TN_PROJ - 1))

    def u_index_sample(i, j, u_first):
        return (0, jnp.clip(j - u_first, 0, WS // TN_PROJ - 1))

    conv_p, re_p, im_p, conv_s, re_s, im_s = [], [], [], [], [], []
    for l in range(depth):
        lb, a2, bb, ab = _discretize(ssm_lambda_re[l], ssm_lambda_im[l], ssm_log_step[l], ssm_b_re[l], ssm_b_im[l])
        bb = bb.reshape(2, SSM_H, SSM_G, SSM_P)
        ab = ab.reshape(2, SSM_H, SSM_G, SSM_P)
        wb_cur = jnp.concatenate([_block_diag_in(bb[0]), _block_diag_in(bb[1])], axis=2)
        wb_prev = jnp.concatenate([_block_diag_in(ab[0]), _block_diag_in(ab[1])], axis=2)
        wb2 = jnp.concatenate([wb_cur, wb_prev], axis=1).astype(BF16)
        wb1 = wb_cur.astype(BF16)
        a2r = _slice_rows(a2)
        a1r = _slice_rows(lb)
        cmat = jnp.concatenate([_block_diag_out(ssm_c_re[l]), -_block_diag_out(ssm_c_im[l])], axis=1).astype(BF16)
        dvec = ssm_d[l].reshape(N_SLICES, 1, LANES)

        proj_p, u_tm = _in_proj(x_p, norm_mix3, w_in, l, (seq, nb * WS), u_index_prompt, "in_proj_prompt")
        ya_p, cst_p = _conv_branch(proj_p, zero_conv, conv_w, w_out_conv, l, TM_CONV, 1, seq // TM_CONV,
                                   "conv_prompt")
        z_p, hs_p = _ssm_prompt(u_tm.reshape(n_p, WS), wb2, a2r, cmat, dvec, nb)
        d_p = _merge(z_p.reshape(seq, nb * WS), lambda i, j: (i % seq_blks, i // seq_blks),
                     ya_p, proj_p, b_gate3, w_glu, w_o, l, "merge_prompt")

        proj_s, u_s = _in_proj(x_s, norm_mix3, w_in, l, (n_s, WS), u_index_sample, "in_proj_sample")
        cinit = jnp.transpose(state_conv[l], (1, 0, 2)).reshape(1, 2 * n_seq, WC)
        ya_s, cst_s = _conv_branch(proj_s, cinit, conv_w, w_out_conv, l, n_s, n_seq, 1, "conv_sample")
        h0 = jnp.concatenate([state_ssm_re[l].reshape(n_seq, N_SLICES, SLICE_STATES),
                              state_ssm_im[l].reshape(n_seq, N_SLICES, SLICE_STATES)], axis=2)
        z_s, hs_s = _ssm_sample(u_s, h0.reshape(n_seq, N_SLICES * SLICE_COLS), wb1, a1r, cmat, dvec,
                                n_seq, n_steps)
        d_s = _merge(z_s, lambda i, j: (0, 0), ya_s, proj_s, b_gate3, w_glu, w_o, l, "merge_sample")

        conv_p.append(cst_p[:, SUBLANES - 2:, :])
        hs_p = hs_p[:, nb:, :].reshape(N_SLICES, nb, 2, GROUPS_PER_SLICE, SSM_P)
        hs_p = jnp.transpose(hs_p, (2, 1, 0, 3, 4)).reshape(2, nb, SSM_G, SSM_P)
        re_p.append(hs_p[0])
        im_p.append(hs_p[1])
        conv_s.append(jnp.transpose(cst_s.reshape(2, n_seq, WC), (1, 0, 2)))
        hs_s = hs_s.reshape(n_seq, N_SLICES, 2, GROUPS_PER_SLICE, SSM_P)
        hs_s = jnp.transpose(hs_s, (2, 0, 1, 3, 4)).reshape(2, n_seq, SSM_G, SSM_P)
        re_s.append(hs_s[0])
        im_s.append(hs_s[1])

        wr = jnp.zeros((LANES, dm), F32)
        wr = wr.at[0:N_GROUPS].set(w_router_group[l].T)
        wr = wr.at[SUBLANES:SUBLANES + N_EXPERTS].set(w_router_expert[l].T)
        whi = wr.astype(BF16)
        wlo = (wr - whi.astype(F32)).astype(BF16)
        br = jnp.zeros((LANES, 1), F32)
        br = br.at[0:N_GROUPS, 0].set(b_router_group[l])
        br = br.at[SUBLANES:SUBLANES + N_EXPERTS, 0].set(b_router_expert[l])
        x_mid, xn, eid, wts, rank, counts = _router(x_p, d_p, x_s, d_s, norm_ffn3, whi, wlo, br, l)
        pos_flat, tile_expert, next_expert, n_used = _routing_plan(eid, rank, counts, n_tiles)
        sorted_token = _invert(pos_flat, t_all, n_tiles * TE)
        y = _experts(tile_expert, next_expert, n_used, sorted_token, xn, w_exp_gate, w_exp_up, w_exp_down, l)
        final = l == depth - 1
        x_p, x_s = _combine(pos_flat, x_mid, jnp.transpose(wts), norm_final.reshape(1, dm), y, final, n_p)

    y_prompt = x_p.reshape(nb, seq, dm)
    y_sample = jnp.transpose(x_s.reshape(n_steps, n_seq, dm), (1, 0, 2))
    return (y_prompt, y_sample,
            jnp.stack(conv_p), jnp.stack(re_p), jnp.stack(im_p),
            jnp.stack(conv_s), jnp.stack(re_s), jnp.stack(im_s))
```
